```python
import math
import jax, jax.numpy as jnp
from jax import lax
import numpy as np

D_MODEL = 2048
BATCH = 4
SEQ = 2048
DEPTH = 2
DEC_BATCH = 128
DEC_SEQ = 8
PAST_LEN = 16384
PAGE_SIZE = 128

POOL_WINDOWS = (2, 4, 8, 16)
POOL_GROUPS = 4
POOL_WIDTH = D_MODEL // 2
POOL_GW = POOL_WIDTH // POOL_GROUPS
POOL_OUT_GW = D_MODEL // POOL_GROUPS
POOL_BUF = max(POOL_WINDOWS) - 1
GLA_HEADS = 4
GLA_DK = D_MODEL // 2 // GLA_HEADS
GLA_DV = D_MODEL // GLA_HEADS
GLA_K = GLA_HEADS * GLA_DK
GLA_V = GLA_HEADS * GLA_DV
GLA_RANK = 16
GLA_TAU = 16.0
GLA_CHUNK = 64
D_FF = 5632
LN_EPS = 1e-5
HEAD_NORM_EPS = 1e-6
ALPHA = (2 * DEPTH) ** 0.25
BETA = (8 * DEPTH) ** -0.25
IN_WIDTH = POOL_WIDTH + 2 * GLA_K + GLA_V + GLA_RANK + GLA_V + 2 * D_MODEL

kernel_name = "hybrid_pool_gla_macaron_deepnorm_step"


def _layer_norm(x, g, b):
    xf = x.astype(jnp.float32)
    mu = jnp.mean(xf, axis=-1, keepdims=True)
    xc = xf - mu
    var = jnp.mean(xc * xc, axis=-1, keepdims=True)
    return (xc * lax.rsqrt(var + LN_EPS) * g + b).astype(x.dtype)


def _swiglu(x, w_in, w_out):
    gate, up = jnp.split(x @ w_in, 2, axis=-1)
    return (jax.nn.silu(gate) * up) @ w_out


def _split_proj(h):
    sizes = (POOL_WIDTH, GLA_K, GLA_K, GLA_V, GLA_RANK, GLA_V, D_MODEL, D_MODEL)
    idx = [int(i) for i in np.cumsum(sizes)[:-1]]
    return jnp.split(h, idx, axis=-1)


def _pool_mixer(u, buf, start, pool_w, pool_scale):
    B, T, P = u.shape
    ext = jnp.concatenate([buf.astype(u.dtype), u], axis=1)
    cs = jnp.cumsum(jnp.concatenate([jnp.zeros((B, 1, P), jnp.float32),
                                     ext.astype(jnp.float32)], axis=1), axis=1)
    off = POOL_BUF + 1
    pos = start + jnp.arange(T)
    outs = []
    for g, w in enumerate(POOL_WINDOWS):
        sl = slice(g * POOL_GW, (g + 1) * POOL_GW)
        win_sum = cs[:, off:off + T, sl] - cs[:, off - w:off - w + T, sl]
        cnt = jnp.minimum(w, pos + 1).astype(jnp.float32)[None, :, None]
        outs.append(win_sum / cnt - u[:, :, sl].astype(jnp.float32))
    pooled = jnp.stack(outs, axis=2).astype(u.dtype)
    y = jnp.einsum('btgc,gcd->btgd', pooled, pool_w).reshape(B, T, D_MODEL) * pool_scale
    return y, ext[:, -POOL_BUF:]


def _gla(q, k, v, loga, S0):
    B, T, H, _ = q.shape
    C = math.gcd(T, GLA_CHUNK)
    n = T // C

    def to_chunks(a):
        return a.astype(jnp.float32).reshape(B, n, C, H, a.shape[-1]).transpose(1, 0, 3, 2, 4)

    mask = jnp.tril(jnp.ones((C, C), bool))[None, None, :, :, None]

    def step(S, inp):
        qc, kc, vc, lc = inp
        b = jnp.cumsum(lc, axis=2)
        o_inter = jnp.einsum('bhik,bhkv->bhiv', qc * jnp.exp(b), S)
        diff = b[:, :, :, None, :] - b[:, :, None, :, :]
        decay = jnp.exp(jnp.where(mask, diff, -jnp.inf))
        A = jnp.einsum('bhik,bhjk,bhijk->bhij', qc, kc, decay)
        o = o_inter + jnp.einsum('bhij,bhjv->bhiv', A, vc)
        b_last = b[:, :, -1:, :]
        S_new = jnp.exp(b_last[:, :, 0, :])[..., None] * S + \
            jnp.einsum('bhjk,bhjv->bhkv', kc * jnp.exp(b_last - b), vc)
        return S_new, o

    S_fin, o = lax.scan(step, S0.astype(jnp.float32),
                        (to_chunks(q), to_chunks(k), to_chunks(v), to_chunks(loga)))
    o = o.transpose(1, 0, 3, 2, 4).reshape(B, T, H, v.shape[-1])
    return o, S_fin.astype(S0.dtype)


def _token_mix(x, pool_buf, gla_state, start, w_in, pool_w, pool_scale, a_up, a_bias,
               head_g, w_gla_out, w_out):
    B, T, _ = x.shape
    u, q, k, v, a_lo, r, g_a, g_b = _split_proj(x @ w_in)
    y_a, new_buf = _pool_mixer(u, pool_buf, start, pool_w, pool_scale)
    loga = jax.nn.log_sigmoid((a_lo @ a_up + a_bias).astype(jnp.float32)) / GLA_TAU
    heads = lambda t, d: t.reshape(B, T, GLA_HEADS, d)
    o, new_S = _gla(heads(q, GLA_DK) * GLA_DK ** -0.5, heads(k, GLA_DK), heads(v, GLA_DV),
                    heads(loga, GLA_DK), gla_state)
    o = o * lax.rsqrt(jnp.mean(o * o, axis=-1, keepdims=True) + HEAD_NORM_EPS)
    o = (o.reshape(B, T, GLA_V) * head_g).astype(x.dtype)
    y_b = (o * jax.nn.silu(r)) @ w_gla_out
    merged = jax.nn.sigmoid(g_a) * y_a + jax.nn.sigmoid(g_b) * y_b
    return merged @ w_out, new_buf, new_S


def _trunk(x, pool_bufs, gla_states, start, ln_g, ln_b, w_ffn_in, w_ffn_out, w_in, pool_w,
           pool_scale, a_up, a_bias, head_g, w_gla_out, w_out):
    new_bufs, new_states = [], []
    for l in range(DEPTH):
        x = _layer_norm(ALPHA * x + 0.5 * _swiglu(x, w_ffn_in[l, 0], w_ffn_out[l, 0]),
                        ln_g[l, 0], ln_b[l, 0])
        m, nb, ns = _token_mix(x, pool_bufs[l], gla_states[l], start, w_in[l], pool_w[l],
                               pool_scale[l], a_up[l], a_bias[l], head_g[l], w_gla_out[l], w_out[l])
        x = _layer_norm(ALPHA * x + m, ln_g[l, 1], ln_b[l, 1])
        x = _layer_norm(ALPHA * x + 0.5 * _swiglu(x, w_ffn_in[l, 1], w_ffn_out[l, 1]),
                        ln_g[l, 2], ln_b[l, 2])
        new_bufs.append(nb)
        new_states.append(ns)
    return x, jnp.stack(new_bufs), jnp.stack(new_states)


def setup_inputs(seed: int = 0) -> dict:
    key = jax.random.key(seed)
    ks = jax.random.split(key, 18)
    nrm = lambda k, shape, scale: jax.random.normal(k, shape, jnp.float32) * scale
    return {
        "x_prompt": nrm(ks[0], (BATCH, SEQ, D_MODEL), 1.0),
        "x_sample": nrm(ks[1], (DEC_BATCH, DEC_SEQ, D_MODEL), 1.0),
        "state_pool": nrm(ks[2], (DEPTH, DEC_BATCH, POOL_BUF, POOL_WIDTH), 1.0),
        "state_gla": nrm(ks[3], (DEPTH, DEC_BATCH, GLA_HEADS, GLA_DK, GLA_DV), 0.5),
        "ln_g": 1.0 + nrm(ks[4], (DEPTH, 3, D_MODEL), 0.02),
        "ln_b": nrm(ks[5], (DEPTH, 3, D_MODEL), 0.01),
        "w_ffn_in": nrm(ks[6], (DEPTH, 2, D_MODEL, 2 * D_FF), D_MODEL ** -0.5),
        "w_ffn_out": nrm(ks[7], (DEPTH, 2, D_FF, D_MODEL), BETA * D_FF ** -0.5),
        "w_in": nrm(ks[8], (DEPTH, D_MODEL, IN_WIDTH), D_MODEL ** -0.5),
        "pool_w": nrm(ks[9], (DEPTH, POOL_GROUPS, POOL_GW, POOL_OUT_GW), POOL_GW ** -0.5),
        "pool_scale": 1.0 + nrm(ks[10], (DEPTH, D_MODEL), 0.02),
        "a_up": nrm(ks[11], (DEPTH, GLA_RANK, GLA_K), GLA_RANK ** -0.5),
        "a_bias": nrm(ks[12], (DEPTH, GLA_K), 0.1),
        "head_g": 1.0 + nrm(ks[13], (DEPTH, GLA_V), 0.02),
        "w_gla_out": nrm(ks[14], (DEPTH, GLA_V, D_MODEL), GLA_V ** -0.5),
        "w_out": nrm(ks[15], (DEPTH, D_MODEL, D_MODEL), BETA * D_MODEL ** -0.5),
    }


def reference(x_prompt, x_sample, state_pool, state_gla, ln_g, ln_b, w_ffn_in, w_ffn_out,
              w_in, pool_w, pool_scale, a_up, a_bias, head_g, w_gla_out, w_out):
    B = x_prompt.shape[0]
    zero_pool = jnp.zeros((DEPTH, B, POOL_BUF, POOL_WIDTH), x_prompt.dtype)
    zero_gla = jnp.zeros((DEPTH, B, GLA_HEADS, GLA_DK, GLA_DV), state_gla.dtype)
    y_prompt, new_pool_prompt, new_gla_prompt = _trunk(
        x_prompt, zero_pool, zero_gla, 0, ln_g, ln_b, w_ffn_in, w_ffn_out, w_in, pool_w,
        pool_scale, a_up, a_bias, head_g, w_gla_out, w_out)
    y_sample, new_pool_sample, new_gla_sample = _trunk(
        x_sample, state_pool, state_gla, PAST_LEN, ln_g, ln_b, w_ffn_in, w_ffn_out, w_in, pool_w,
        pool_scale, a_up, a_bias, head_g, w_gla_out, w_out)
    return (y_prompt, y_sample, new_pool_prompt, new_gla_prompt, new_pool_sample, new_gla_sample)
```

```python
import functools

import numpy as np
import jax
import jax.numpy as jnp
from jax import lax
from jax.experimental import pallas as pl
from jax.experimental.pallas import tpu as pltpu

POOL_WINDOWS = (2, 4, 8, 16)
POOL_BUF = max(POOL_WINDOWS) - 1
GLA_TAU = 16.0
LN_EPS = 1e-5
HEAD_NORM_EPS = 1e-6
PAST_LEN = 16384

V7X_LANES = 128
V7X_SUBLANES = 8
V7X_VMEM_LIMIT_BYTES = 56 * 1024 * 1024

F32 = jnp.float32
BF16 = jnp.bfloat16

GLA_SUB = 16
GLA_CHUNK_ROWS = 128


def _cparams(sem):
    return pltpu.CompilerParams(dimension_semantics=sem, vmem_limit_bytes=V7X_VMEM_LIMIT_BYTES)


def _layer_norm(y, g, b):
    mu = jnp.mean(y, axis=-1, keepdims=True)
    yc = y - mu
    var = jnp.mean(yc * yc, axis=-1, keepdims=True)
    return yc * lax.rsqrt(var + LN_EPS) * g + b


def _dot(a, b):
    return jnp.dot(a, b, preferred_element_type=F32)


def _dot_nt(a, b):
    return lax.dot_general(a, b, (((1,), (1,)), ((), ())), preferred_element_type=F32)


def _split3(x):
    p0 = x.astype(BF16)
    r1 = x - p0.astype(F32)
    p1 = r1.astype(BF16)
    p2 = (r1 - p1.astype(F32)).astype(BF16)
    return p0, p1, p2


def _ffn_ln_kernel(x_ref, wg_ref, wu_ref, wo_ref, g_ref, b_ref, o_ref, xb_ref, acc_ref, *, alpha):
    j = pl.program_id(1)

    @pl.when(j == 0)
    def _():
        xb_ref[...] = x_ref[...].astype(BF16)
        acc_ref[...] = jnp.zeros_like(acc_ref)

    xb = xb_ref[...]
    gate = _dot(xb, wg_ref[...])
    up = _dot(xb, wu_ref[...])
    act = (gate * jax.nn.sigmoid(gate) * up).astype(BF16)
    acc_ref[...] += _dot(act, wo_ref[...])

    @pl.when(j == pl.num_programs(1) - 1)
    def _():
        y = alpha * x_ref[...] + 0.5 * acc_ref[...]
        o_ref[...] = _layer_norm(y, g_ref[...], b_ref[...])


def _ffn_ln(x, w_in_b, w_out_b, ln_g, ln_b, layer, which, ln_idx, alpha, tm, tf):
    T, D = x.shape
    F = w_out_b.shape[2]
    nf = F // tf
    return pl.pallas_call(
        functools.partial(_ffn_ln_kernel, alpha=alpha),
        grid=(T // tm, nf),
        in_specs=[
            pl.BlockSpec((tm, D), lambda i, j: (i, 0)),
            pl.BlockSpec((None, None, D, tf), lambda i, j: (layer, which, 0, j)),
            pl.BlockSpec((None, None, D, tf), lambda i, j: (layer, which, 0, j + nf)),
            pl.BlockSpec((None, None, tf, D), lambda i, j: (layer, which, j, 0)),
            pl.BlockSpec((None, None, 1, D), lambda i, j: (layer, ln_idx, 0, 0)),
            pl.BlockSpec((None, None, 1, D), lambda i, j: (layer, ln_idx, 0, 0)),
        ],
        out_specs=pl.BlockSpec((tm, D), lambda i, j: (i, 0)),
        out_shape=jax.ShapeDtypeStruct((T, D), F32),
        scratch_shapes=[pltpu.VMEM((tm, D), BF16), pltpu.VMEM((tm, D), F32)],
        compiler_params=_cparams(("parallel", "arbitrary")),
        name="ffn_ln",
    )(x, w_in_b, w_in_b, w_out_b, ln_g, ln_b)


def _proj_kernel(x_ref, wm_ref, wa_ref, aup_ref, ab_ref, h_ref, la_ref, xb_ref):
    j = pl.program_id(1)

    @pl.when(j == 0)
    def _():
        xb = x_ref[...].astype(BF16)
        xb_ref[...] = xb
        a_lo = _dot(xb, wa_ref[...])
        z = _dot(a_lo.astype(BF16), aup_ref[...]) + ab_ref[...]
        la_ref[...] = (jnp.minimum(z, 0.0) - jnp.log1p(jnp.exp(-jnp.abs(z)))) / GLA_TAU

    h_ref[...] = _dot(xb_ref[...], wm_ref[...])


def _proj(x, wm, wa, aup, abias, layer, tm, tn):
    T, D = x.shape
    NH = wm.shape[2]
    GK = aup.shape[2]
    return pl.pallas_call(
        _proj_kernel,
        grid=(T // tm, NH // tn),
        in_specs=[
            pl.BlockSpec((tm, D), lambda i, j: (i, 0)),
            pl.BlockSpec((None, D, tn), lambda i, j: (layer, 0, j)),
            pl.BlockSpec((None, D, V7X_LANES), lambda i, j: (layer, 0, 0)),
            pl.BlockSpec((None, V7X_LANES, GK), lambda i, j: (layer, 0, 0)),
            pl.BlockSpec((None, 1, GK), lambda i, j: (layer, 0, 0)),
        ],
        out_specs=[
            pl.BlockSpec((tm, tn), lambda i, j: (i, j)),
            pl.BlockSpec((tm, GK), lambda i, j: (i, 0)),
        ],
        out_shape=[jax.ShapeDtypeStruct((T, NH), F32), jax.ShapeDtypeStruct((T, GK), F32)],
        scratch_shapes=[pltpu.VMEM((tm, D), BF16)],
        compiler_params=_cparams(("parallel", "arbitrary")),
        name="proj",
    )(x, wm, wa, aup, abias)


def _stack01(mats, dtype):
    return jnp.asarray(np.concatenate(mats, axis=0).astype(np.float32), dtype=dtype)


def _cumsum_mats(rows, groups):
    i = np.arange(rows)[:, None]
    j = np.arange(rows)[None, :]
    mats = [j <= i]
    for grp in groups:
        mats.append(j <= (i // grp) * grp + grp // 2 - 1)
    return _stack01(mats, BF16)


def _seq_cumsum_mats(rows, seq):
    i = np.arange(rows)[:, None]
    j = np.arange(rows)[None, :]
    same = i // seq == j // seq
    return _stack01([same & (j <= i), same], BF16)


def _level_masks(rows, groups):
    i = np.arange(rows)[:, None]
    j = np.arange(rows)[None, :]
    out = []
    for grp in groups:
        half = grp // 2
        out.append((i // grp == j // grp) & (i % grp >= half) & (j % grp < half))
    return _stack01(out, F32)


def _intra_chunk_scores(qs, k, b, refs, lmask_ref, sub):
    C = qs.shape[0]
    A = jnp.zeros((C, C), F32)
    for n, r in enumerate(refs):
        qh = (qs * jnp.exp(jnp.minimum(b - r, 0.0))).astype(BF16)
        kh = (k * jnp.exp(jnp.minimum(r - b, 0.0))).astype(BF16)
        A = A + _dot_nt(qh, kh) * lmask_ref[n * C:(n + 1) * C, :]
    row_in_sub = lax.broadcasted_iota(jnp.int32, (C, 1), 0) & (sub - 1)
    delta = lax.broadcasted_iota(jnp.int32, (C, C), 0) - lax.broadcasted_iota(jnp.int32, (C, C), 1)
    for d in range(sub):
        if d == 0:
            t = qs * k
        else:
            kd = pltpu.roll(k, d, axis=0)
            bd = pltpu.roll(b, d, axis=0)
            t = qs * kd * jnp.exp(jnp.minimum(b - bd, 0.0))
        red = jnp.sum(t, axis=1, keepdims=True)
        red = jnp.where(row_in_sub >= d, red, 0.0)
        A = A + jnp.where(delta == d, red, 0.0)
    return A


def _head_norm(o):
    return o * lax.rsqrt(jnp.mean(o * o, axis=-1, keepdims=True) + HEAD_NORM_EPS)


def _gla_prompt_kernel(q_ref, k_ref, v_ref, la_ref, cmat_ref, lmask_ref, o_ref, sout_ref, s_ref,
                       *, scale, sub):
    t = pl.program_id(2)
    C, DK = q_ref.shape
    DV = v_ref.shape[1]

    @pl.when(t == 0)
    def _():
        s_ref[...] = jnp.zeros_like(s_ref)

    qs = q_ref[...] * scale
    k = k_ref[...]
    vb = v_ref[...].astype(BF16)
    cm = cmat_ref[...]
    p0, p1, p2 = _split3(la_ref[...])
    ball = _dot(cm, p0) + _dot(cm, p1) + _dot(cm, p2)
    b = ball[0:C]
    nlev = cm.shape[0] // C - 1
    refs = [ball[(n + 1) * C:(n + 2) * C] for n in range(nlev)]

    A = _intra_chunk_scores(qs, k, b, refs, lmask_ref, sub)
    S = s_ref[...]
    o = _dot((qs * jnp.exp(b)).astype(BF16), S.astype(BF16)) + _dot(A.astype(BF16), vb)
    o_ref[...] = _head_norm(o)

    bl = b[C - 1:C, :]
    ku_t = (k * jnp.exp(bl - b)).T
    d_s = _dot(ku_t.astype(BF16), vb)
    a_t = jnp.exp(jnp.broadcast_to(bl, (V7X_LANES, DK))).T
    for n in range(DV // V7X_LANES):
        sl = slice(n * V7X_LANES, (n + 1) * V7X_LANES)
        s_ref[:, sl] = a_t * S[:, sl] + d_s[:, sl]

    @pl.when(t == pl.num_programs(2) - 1)
    def _():
        sout_ref[...] = s_ref[...]


def _gla_prompt(h, la, batch, seq, heads, dk, dv, q_off, k_off, v_off):
    C = GLA_CHUNK_ROWS
    nt = seq // C
    groups = []
    g = C
    while g > GLA_SUB:
        groups.append(g)
        g //= 2
    cmat = _cumsum_mats(C, groups)
    lmask = _level_masks(C, groups)
    qb, kb, vb = q_off // dk, k_off // dk, v_off // dv
    return pl.pallas_call(
        functools.partial(_gla_prompt_kernel, scale=float(dk) ** -0.5, sub=GLA_SUB),
        grid=(batch, heads, nt),
        in_specs=[
            pl.BlockSpec((C, dk), lambda bi, hi, t: (bi * nt + t, qb + hi)),
            pl.BlockSpec((C, dk), lambda bi, hi, t: (bi * nt + t, kb + hi)),
            pl.BlockSpec((C, dv), lambda bi, hi, t: (bi * nt + t, vb + hi)),
            pl.BlockSpec((C, dk), lambda bi, hi, t: (bi * nt + t, hi)),
            pl.BlockSpec(cmat.shape, lambda bi, hi, t: (0, 0)),
            pl.BlockSpec(lmask.shape, lambda bi, hi, t: (0, 0)),
        ],
        out_specs=[
            pl.BlockSpec((C, dv), lambda bi, hi, t: (bi * nt + t, hi)),
            pl.BlockSpec((None, None, dk, dv), lambda bi, hi, t: (bi, hi, 0, 0)),
        ],
        out_shape=[
            jax.ShapeDtypeStruct((batch * seq, heads * dv), F32),
            jax.ShapeDtypeStruct((batch, heads, dk, dv), F32),
        ],
        scratch_shapes=[pltpu.VMEM((dk, dv), F32)],
        compiler_params=_cparams(("parallel", "parallel", "arbitrary")),
        name="gla_prompt",
    )(h, h, h, la, cmat, lmask)


def _gla_sample_kernel(q_ref, k_ref, v_ref, la_ref, cmat_ref, sin_ref, o_ref, sout_ref, *, scale, seq):
    R, DK = q_ref.shape
    DV = v_ref.shape[1]
    nb = R // seq

    qs = q_ref[...] * scale
    k = k_ref[...]
    vb = v_ref[...].astype(BF16)
    cm = cmat_ref[...]
    p0, p1, p2 = _split3(la_ref[...])
    ball = _dot(cm, p0) + _dot(cm, p1) + _dot(cm, p2)
    b = ball[0:R]
    bl = ball[R:2 * R]

    A = _intra_chunk_scores(qs, k, b, [], None, seq)
    o_intra = _dot(A.astype(BF16), vb)
    qo = qs * jnp.exp(b)
    ku_t = (k * jnp.exp(bl - b)).T
    lane_seq = lax.broadcasted_iota(jnp.int32, (DK, R), 1) // seq
    for i in range(nb):
        rows = slice(i * seq, (i + 1) * seq)
        S = sin_ref[i]
        o = _dot(qo[rows].astype(BF16), S.astype(BF16)) + o_intra[rows]
        o_ref[rows, :] = _head_norm(o)
        d_s = _dot(jnp.where(lane_seq == i, ku_t, 0.0).astype(BF16), vb)
        a_t = jnp.exp(jnp.broadcast_to(bl[i * seq:i * seq + 1, :], (V7X_LANES, DK))).T
        for n in range(DV // V7X_LANES):
            sl = slice(n * V7X_LANES, (n + 1) * V7X_LANES)
            sout_ref[i, :, sl] = a_t * S[:, sl] + d_s[:, sl]


def _gla_sample(h, la, state, layer, row0, batch, seq, heads, dk, dv, q_off, k_off, v_off):
    R = V7X_LANES
    nb = R // seq
    ng = batch // nb
    rb = row0 // R
    cmat = _seq_cumsum_mats(R, seq)
    qb, kb, vb = q_off // dk, k_off // dk, v_off // dv
    return pl.pallas_call(
        functools.partial(_gla_sample_kernel, scale=float(dk) ** -0.5, seq=seq),
        grid=(ng, heads),
        in_specs=[
            pl.BlockSpec((R, dk), lambda g, hi: (rb + g, qb + hi)),
            pl.BlockSpec((R, dk), lambda g, hi: (rb + g, kb + hi)),
            pl.BlockSpec((R, dv), lambda g, hi: (rb + g, vb + hi)),
            pl.BlockSpec((R, dk), lambda g, hi: (rb + g, hi)),
            pl.BlockSpec(cmat.shape, lambda g, hi: (0, 0)),
            pl.BlockSpec((None, nb, None, dk, dv), lambda g, hi: (layer, g, hi, 0, 0)),
        ],
        out_specs=[
            pl.BlockSpec((R, dv), lambda g, hi: (g, hi)),
            pl.BlockSpec((nb, None, dk, dv), lambda g, hi: (g, hi, 0, 0)),
        ],
        out_shape=[
            jax.ShapeDtypeStruct((batch * seq, heads * dv), F32),
            jax.ShapeDtypeStruct((batch, heads, dk, dv), F32),
        ],
        compiler_params=_cparams(("parallel", "parallel")),
        name="gla_sample",
    )(h, h, h, la, cmat, state)


def _window_means_minus_input(ext, u, pos, lead):
    n, pw = u.shape
    gw = pw // len(POOL_WINDOWS)
    outs = []
    for g, w in enumerate(POOL_WINDOWS):
        cols = slice(g * gw, (g + 1) * gw)
        p = ext[:, cols]
        span = 1
        while span < w:
            p = p + pltpu.roll(p, span, axis=0)
            span *= 2
        cnt = jnp.minimum(w, pos + 1).astype(F32)
        outs.append(p[lead:lead + n] / cnt - u[:, cols])
    return outs


def _mix_tail(pooled, on, r, ga, gb, x, hg_ref, wgo_ref, pw_ref, ps_ref, wo_ref, g_ref, b_ref, alpha):
    ya = jnp.concatenate(
        [_dot(p.astype(BF16), pw_ref[g]) for g, p in enumerate(pooled)], axis=1) * ps_ref[...]
    yb = _dot((on * hg_ref[...] * (r * jax.nn.sigmoid(r))).astype(BF16), wgo_ref[...])
    merged = jax.nn.sigmoid(ga) * ya + jax.nn.sigmoid(gb) * yb
    m = _dot(merged.astype(BF16), wo_ref[...])
    return _layer_norm(alpha * x + m, g_ref[...], b_ref[...])


def _post_prompt_kernel(on_ref, r_ref, ga_ref, gb_ref, u_ref, x_ref, hg_ref, wgo_ref, pw_ref, ps_ref,
                        wo_ref, g_ref, b_ref, o_ref, halo_ref, *, tiles_per_seq, alpha):
    i = pl.program_id(0)
    tm = u_ref.shape[0]
    lead = halo_ref.shape[0]
    ti = i % tiles_per_seq

    @pl.when(ti == 0)
    def _():
        halo_ref[...] = jnp.zeros_like(halo_ref)

    u = u_ref[...]
    ext = jnp.concatenate([halo_ref[...], u], axis=0)
    halo_ref[...] = u[tm - lead:, :]
    pos = ti * tm + lax.broadcasted_iota(jnp.int32, (tm, 1), 0)
    pooled = _window_means_minus_input(ext, u, pos, lead)
    o_ref[...] = _mix_tail(pooled, on_ref[...], r_ref[...], ga_ref[...], gb_ref[...], x_ref[...],
                           hg_ref, wgo_ref, pw_ref, ps_ref, wo_ref, g_ref, b_ref, alpha)


def _post_sample_kernel(on_ref, r_ref, ga_ref, gb_ref, u_ref, x_ref, st_ref, hg_ref, wgo_ref, pw_ref,
                        ps_ref, wo_ref, g_ref, b_ref, o_ref, npool_ref, ext_ref, *, seq, start, alpha):
    tm, pw = u_ref.shape
    nb = tm // seq
    lead = ext_ref.shape[1] - seq
    u = u_ref[...]
    u3 = u.reshape(nb, seq, pw)
    st = st_ref[...]
    ext_ref[:, 0:lead - POOL_BUF, :] = jnp.zeros((nb, lead - POOL_BUF, pw), F32)
    ext_ref[:, lead - POOL_BUF:lead, :] = st
    ext_ref[:, lead:, :] = u3
    e3 = ext_ref[...]
    npool_ref[...] = e3[:, lead + seq - POOL_BUF:, :]
    p = e3.reshape(nb * (lead + seq), pw)
    gw = pw // len(POOL_WINDOWS)
    pos = start + lax.broadcasted_iota(jnp.int32, (nb, seq, 1), 1)
    pooled = []
    for g, w in enumerate(POOL_WINDOWS):
        cols = slice(g * gw, (g + 1) * gw)
        pg = p[:, cols]
        span = 1
        while span < w:
            pg = pg + pltpu.roll(pg, span, axis=0)
            span *= 2
        win = pg.reshape(nb, lead + seq, gw)[:, lead:, :]
        cnt = jnp.minimum(w, pos + 1).astype(F32)
        pooled.append((win / cnt - u3[:, :, cols]).reshape(tm, gw))
    o_ref[...] = _mix_tail(pooled, on_ref[...], r_ref[...], ga_ref[...], gb_ref[...], x_ref[...],
                           hg_ref, wgo_ref, pw_ref, ps_ref, wo_ref, g_ref, b_ref, alpha)


def _const_spec(shape, index_map):
    return pl.BlockSpec(shape, index_map, pipeline_mode=pl.Buffered(1))


def _post_weight_specs(layer, ln_idx, D, GV, groups, gw, ogw, n):
    z = (0,) * (n - 1)
    return [
        _const_spec((None, 1, GV), lambda *a: (layer, 0, 0)),
        _const_spec((None, GV, D), lambda *a: (layer, 0, 0)),
        _const_spec((None, groups, gw, ogw), lambda *a: (layer, 0, 0, 0)),
        _const_spec((None, 1, D), lambda *a: (layer, 0, 0)),
        _const_spec((None, D, D), lambda *a: (layer, 0, 0)),
        _const_spec((None, None, 1, D), lambda *a: (layer, ln_idx, 0, 0)),
        _const_spec((None, None, 1, D), lambda *a: (layer, ln_idx, 0, 0)),
    ]


def _post_prompt(on, h, x, weights, layer, ln_idx, alpha, seq, tm, cols):
    TP, GV = on.shape
    D = x.shape[1]
    hg, wgo, pw, ps, wo, ln_g, ln_b = weights
    groups, gw, ogw = pw.shape[1:]
    PW = groups * gw
    r_b, ga_b, gb_b, u_b = cols["r"] // GV, cols["ga"] // D, cols["gb"] // D, cols["u"] // PW
    lead = 2 * V7X_SUBLANES
    return pl.pallas_call(
        functools.partial(_post_prompt_kernel, tiles_per_seq=seq // tm, alpha=alpha),
        grid=(TP // tm,),
        in_specs=[
            pl.BlockSpec((tm, GV), lambda i: (i, 0)),
            pl.BlockSpec((tm, GV), lambda i: (i, r_b)),
            pl.BlockSpec((tm, D), lambda i: (i, ga_b)),
            pl.BlockSpec((tm, D), lambda i: (i, gb_b)),
            pl.BlockSpec((tm, PW), lambda i: (i, u_b)),
            pl.BlockSpec((tm, D), lambda i: (i, 0)),
        ] + _post_weight_specs(layer, ln_idx, D, GV, groups, gw, ogw, 1),
        out_specs=pl.BlockSpec((tm, D), lambda i: (i, 0)),
        out_shape=jax.ShapeDtypeStruct((TP, D), F32),
        scratch_shapes=[pltpu.VMEM((lead, PW), F32)],
        compiler_params=_cparams(("arbitrary",)),
        name="post_prompt",
    )(on, h, h, h, h, x, hg, wgo, pw, ps, wo, ln_g, ln_b)


def _post_sample(on, h, x, state_pool, weights, layer, ln_idx, alpha, row0, seq, tm, cols):
    TS, GV = on.shape
    D = x.shape[1]
    hg, wgo, pw, ps, wo, ln_g, ln_b = weights
    groups, gw, ogw = pw.shape[1:]
    PW = groups * gw
    nb = tm // seq
    batch = TS // seq
    rb = row0 // tm
    r_b, ga_b, gb_b, u_b = cols["r"] // GV, cols["ga"] // D, cols["gb"] // D, cols["u"] // PW
    lead = 2 * V7X_SUBLANES
    return pl.pallas_call(
        functools.partial(_post_sample_kernel, seq=seq, start=PAST_LEN, alpha=alpha),
        grid=(TS // tm,),
        in_specs=[
            pl.BlockSpec((tm, GV), lambda i: (i, 0)),
            pl.BlockSpec((tm, GV), lambda i: (rb + i, r_b)),
            pl.BlockSpec((tm, D), lambda i: (rb + i, ga_b)),
            pl.BlockSpec((tm, D), lambda i: (rb + i, gb_b)),
            pl.BlockSpec((tm, PW), lambda i: (rb + i, u_b)),
            pl.BlockSpec((tm, D), lambda i: (rb + i, 0)),
            pl.BlockSpec((None, nb, POOL_BUF, PW), lambda i: (layer, i, 0, 0)),
        ] + _post_weight_specs(layer, ln_idx, D, GV, groups, gw, ogw, 1),
        out_specs=[
            pl.BlockSpec((tm, D), lambda i: (i, 0)),
            pl.BlockSpec((nb, POOL_BUF, PW), lambda i: (i, 0, 0)),
        ],
        out_shape=[
            jax.ShapeDtypeStruct((TS, D), F32),
            jax.ShapeDtypeStruct((batch, POOL_BUF, PW), F32),
        ],
        scratch_shapes=[pltpu.VMEM((nb, lead + seq, PW), F32)],
        compiler_params=_cparams(("arbitrary",)),
        name="post_sample",
    )(on, h, h, h, h, x, state_pool, hg, wgo, pw, ps, wo, ln_g, ln_b)


def _tiles(T, TP, seq_p, F, NH):
    def fit(n, t):
        while n % t:
            t //= 2
        return t
    return dict(
        ffn_tm=fit(np.gcd(T, TP), 512), ffn_tf=fit(F, 512),
        proj_tm=fit(np.gcd(T, TP), 1024), proj_tn=fit(NH, 1024),
        post_tm=fit(seq_p, 256), post_sample_tm=fit(np.gcd(T - TP, TP), 128),
    )


def kernel(x_prompt, x_sample, state_pool, state_gla, ln_g, ln_b, w_ffn_in, w_ffn_out, w_in, pool_w,
           pool_scale, a_up, a_bias, head_g, w_gla_out, w_out):
    B, SEQ, D = x_prompt.shape
    DB, DSEQ, _ = x_sample.shape
    L = w_in.shape[0]
    PW = state_pool.shape[-1]
    H, DK, DV = state_gla.shape[2:]
    GK, GV = H * DK, H * DV
    RANK = a_up.shape[1]
    F = w_ffn_out.shape[2]
    TP, TS = B * SEQ, DB * DSEQ
    T = TP + TS
    alpha = float((2 * L) ** 0.25)
    assert SEQ % GLA_CHUNK_ROWS == 0 and TP % V7X_LANES == 0 and V7X_LANES % DSEQ == 0
    assert DSEQ == V7X_SUBLANES and PAST_LEN >= POOL_BUF

    lo0 = PW + 2 * GK + GV
    lo1 = lo0 + RANK
    wm = jnp.concatenate([w_in[:, :, lo1:], w_in[:, :, PW + 2 * GK:lo0], w_in[:, :, :PW + 2 * GK]],
                         axis=-1).astype(BF16)
    wa = jnp.pad(w_in[:, :, lo0:lo1], ((0, 0), (0, 0), (0, V7X_LANES - RANK))).astype(BF16)
    aup = jnp.pad(a_up, ((0, 0), (0, V7X_LANES - RANK), (0, 0))).astype(BF16)
    abias = a_bias.reshape(L, 1, GK)
    wf_in = w_ffn_in.astype(BF16)
    wf_out = w_ffn_out.astype(BF16)
    ln_g4 = ln_g.reshape(L, 3, 1, D)
    ln_b4 = ln_b.reshape(L, 3, 1, D)
    post_w = (head_g.reshape(L, 1, GV), w_gla_out.astype(BF16), pool_w.astype(BF16),
              pool_scale.reshape(L, 1, D), w_out.astype(BF16), ln_g4, ln_b4)
    c_v = GV + 2 * D
    c_u = c_v + GV
    cols = dict(r=0, ga=GV, gb=GV + D, v=c_v, u=c_u, q=c_u + PW, k=c_u + PW + GK)
    ts = _tiles(T, TP, SEQ, F, wm.shape[2])
    assert GV == D and c_v % DV == 0 and c_u % PW == 0 and cols["q"] % DK == 0

    x = jnp.concatenate([x_prompt.reshape(TP, D), x_sample.reshape(TS, D)], axis=0)
    pool_p, gla_p, pool_s, gla_s = [], [], [], []
    for l in range(L):
        x = _ffn_ln(x, wf_in, wf_out, ln_g4, ln_b4, l, 0, 0, alpha, ts["ffn_tm"], ts["ffn_tf"])
        h, la = _proj(x, wm, wa, aup, abias, l, ts["proj_tm"], ts["proj_tn"])
        on_p, s_p = _gla_prompt(h, la, B, SEQ, H, DK, DV, cols["q"], cols["k"], cols["v"])
        on_s, s_s = _gla_sample(h, la, state_gla, l, TP, DB, DSEQ, H, DK, DV,
                                cols["q"], cols["k"], cols["v"])
        x_p = _post_prompt(on_p, h, x, post_w, l, 1, alpha, SEQ, ts["post_tm"], cols)
        x_s, np_s = _post_sample(on_s, h, x, state_pool, post_w, l, 1, alpha, TP, DSEQ,
                                 ts["post_sample_tm"], cols)
        x = jnp.concatenate([x_p, x_s], axis=0)
        x = _ffn_ln(x, wf_in, wf_out, ln_g4, ln_b4, l, 1, 2, alpha, ts["ffn_tm"], ts["ffn_tf"])
        pool_p.append(h[:TP, c_u:c_u + PW].reshape(B, SEQ, PW)[:, SEQ - POOL_BUF:, :])
        gla_p.append(s_p)
        pool_s.append(np_s)
        gla_s.append(s_s)
    return (x[:TP].reshape(B, SEQ, D), x[TP:].reshape(DB, DSEQ, D), jnp.stack(pool_p),
            jnp.stack(gla_p), jnp.stack(pool_s), jnp.stack(gla_s))
```

```python
import functools

import numpy as np
import jax
import jax.numpy as jnp
from jax import lax
from jax.experimental import pallas as pl
from jax.experimental.pallas import tpu as pltpu

POOL_WINDOWS = (2, 4, 8, 16)
POOL_BUF = max(POOL_WINDOWS) - 1
GLA_TAU = 16.0
LN_EPS = 1e-5
HEAD_NORM_EPS = 1e-6
PAST_LEN = 16384

V7X_LANES = 128
V7X_SUBLANES = 8
V7X_VMEM_LIMIT_BYTES = 56 * 1024 * 1024

F32 = jnp.float32
BF16 = jnp.bfloat16

GLA_SUB = 4
GLA_CHUNK_ROWS = 128
GLA_HEADS_PER_STEP = 4
LOG2_E = 1.4426950408889634


def _cparams(sem):
    return pltpu.CompilerParams(dimension_semantics=sem, vmem_limit_bytes=V7X_VMEM_LIMIT_BYTES)


def _layer_norm(y, g, b):
    mu = jnp.mean(y, axis=-1, keepdims=True)
    yc = y - mu
    var = jnp.mean(yc * yc, axis=-1, keepdims=True)
    return yc * lax.rsqrt(var + LN_EPS) * g + b


def _dot(a, b):
    return jnp.dot(a, b, preferred_element_type=F32)


def _dot_nt(a, b):
    return lax.dot_general(a, b, (((1,), (1,)), ((), ())), preferred_element_type=F32)


def _split3(x):
    p0 = x.astype(BF16)
    r1 = x - p0.astype(F32)
    p1 = r1.astype(BF16)
    p2 = (r1 - p1.astype(F32)).astype(BF16)
    return p0, p1, p2


def _ffn_ln_kernel(*refs, alpha, n_in, n_out, n_p):
    x_refs = refs[:n_in]
    wg_ref, wu_ref, wo_ref, g_ref, b_ref = refs[n_in:n_in + 5]
    o_refs = refs[n_in + 5:n_in + 5 + n_out]
    xb_ref, acc_ref = refs[n_in + 5 + n_out:]
    i = pl.program_id(0)
    j = pl.program_id(1)
    in_prompt = i < n_p

    def per_group(fn):
        if n_in == 1 and n_out == 1:
            fn(x_refs[0], o_refs[0])
        else:
            pl.when(in_prompt)(lambda: fn(x_refs[0], o_refs[0]))
            pl.when(jnp.logical_not(in_prompt))(lambda: fn(x_refs[-1], o_refs[-1]))

    @pl.when(j == 0)
    def _():
        def load(x_ref, _):
            xb_ref[...] = x_ref[...].astype(BF16)
        per_group(load)
        acc_ref[...] = jnp.zeros_like(acc_ref)

    xb = xb_ref[...]
    gate = _dot(xb, wg_ref[...])
    up = _dot(xb, wu_ref[...])
    act = (gate * jax.nn.sigmoid(gate) * up).astype(BF16)
    acc_ref[...] += _dot(act, wo_ref[...])

    @pl.when(j == pl.num_programs(1) - 1)
    def _():
        def finish(x_ref, o_ref):
            y = alpha * x_ref[...] + 0.5 * acc_ref[...]
            o_ref[...] = _layer_norm(y, g_ref[...], b_ref[...])
        per_group(finish)


def _ffn_ln(xs, out_rows, w_in_b, w_out_b, ln_g, ln_b, layer, which, ln_idx, alpha, tm, tf):
    D = xs[0].shape[1]
    T = sum(x.shape[0] for x in xs)
    F = w_out_b.shape[2]
    nf = F // tf
    n_p = (xs[0].shape[0] if len(xs) == 2 else out_rows[0]) // tm

    def row_specs(n):
        if n == 1:
            return [pl.BlockSpec((tm, D), lambda i, j: (i, 0))]
        return [pl.BlockSpec((tm, D), lambda i, j: (jnp.minimum(i, n_p - 1), 0)),
                pl.BlockSpec((tm, D), lambda i, j: (jnp.maximum(i - n_p, 0), 0))]

    outs = pl.pallas_call(
        functools.partial(_ffn_ln_kernel, alpha=alpha, n_in=len(xs), n_out=len(out_rows), n_p=n_p),
        grid=(T // tm, nf),
        in_specs=row_specs(len(xs)) + [
            pl.BlockSpec((None, None, D, tf), lambda i, j: (layer, which, 0, j)),
            pl.BlockSpec((None, None, D, tf), lambda i, j: (layer, which, 0, j + nf)),
            pl.BlockSpec((None, None, tf, D), lambda i, j: (layer, which, j, 0)),
            pl.BlockSpec((None, None, 1, D), lambda i, j: (layer, ln_idx, 0, 0)),
            pl.BlockSpec((None, None, 1, D), lambda i, j: (layer, ln_idx, 0, 0)),
        ],
        out_specs=row_specs(len(out_rows)),
        out_shape=[jax.ShapeDtypeStruct((r, D), F32) for r in out_rows],
        scratch_shapes=[pltpu.VMEM((tm, D), BF16), pltpu.VMEM((tm, D), F32)],
        compiler_params=_cparams(("arbitrary", "arbitrary")),
        name="ffn_ln",
    )(*xs, w_in_b, w_in_b, w_out_b, ln_g, ln_b)
    return list(outs)


def _proj_kernel(x_ref, wm_ref, wa_ref, aup_ref, ab_ref, h_ref, la_ref, xb_ref):
    j = pl.program_id(1)

    @pl.when(j == 0)
    def _():
        xb = x_ref[...].astype(BF16)
        xb_ref[...] = xb
        a_lo = _dot(xb, wa_ref[...])
        z = _dot(a_lo.astype(BF16), aup_ref[...]) + ab_ref[...]
        la_ref[...] = (jnp.minimum(z, 0.0) - jnp.log1p(jnp.exp(-jnp.abs(z)))) / GLA_TAU

    h_ref[...] = _dot(xb_ref[...], wm_ref[...])


def _proj(x, wm, wa, aup, abias, layer, tm, tn):
    T, D = x.shape
    NH = wm.shape[2]
    GK = aup.shape[2]
    return pl.pallas_call(
        _proj_kernel,
        grid=(T // tm, NH // tn),
        in_specs=[
            pl.BlockSpec((tm, D), lambda i, j: (i, 0)),
            pl.BlockSpec((None, D, tn), lambda i, j: (layer, 0, j)),
            pl.BlockSpec((None, D, V7X_LANES), lambda i, j: (layer, 0, 0)),
            pl.BlockSpec((None, V7X_LANES, GK), lambda i, j: (layer, 0, 0)),
            pl.BlockSpec((None, 1, GK), lambda i, j: (layer, 0, 0)),
        ],
        out_specs=[
            pl.BlockSpec((tm, tn), lambda i, j: (i, j)),
            pl.BlockSpec((tm, GK), lambda i, j: (i, 0)),
        ],
        out_shape=[jax.ShapeDtypeStruct((T, NH), F32), jax.ShapeDtypeStruct((T, GK), F32)],
        scratch_shapes=[pltpu.VMEM((tm, D), BF16)],
        compiler_params=_cparams(("parallel", "arbitrary")),
        name="proj",
    )(x, wm, wa, aup, abias)


def _stack01(mats, dtype):
    return jnp.asarray(np.concatenate(mats, axis=0).astype(np.float32), dtype=dtype)


def _cumsum_mat(rows):
    i = np.arange(rows)[:, None]
    j = np.arange(rows)[None, :]
    return _stack01([j <= i], BF16)


def _seq_cumsum_mats(rows, seq):
    i = np.arange(rows)[:, None]
    j = np.arange(rows)[None, :]
    same = i // seq == j // seq
    return _stack01([same & (j <= i), same], BF16)


def _level_masks(rows, groups):
    i = np.arange(rows)[:, None]
    j = np.arange(rows)[None, :]
    out = []
    for grp in groups:
        half = grp // 2
        out.append((i // grp == j // grp) & (i % grp >= half) & (j % grp < half))
    return _stack01(out, F32)


def _intra_chunk_scores(qs, k, b, groups, lmask_ref, sub):
    C = qs.shape[0]
    A = jnp.zeros((C, C), F32)
    for n, grp in enumerate(groups):
        e = jnp.concatenate(
            [jnp.exp2(-jnp.abs(b[s:s + grp] - b[s + grp // 2 - 1:s + grp // 2])) for s in range(0, C, grp)],
            axis=0)
        A = A + _dot_nt((qs * e).astype(BF16), (k * e).astype(BF16)) * lmask_ref[n * C:(n + 1) * C, :]
    row_in_sub = lax.broadcasted_iota(jnp.int32, (C, 1), 0) & (sub - 1)
    delta = lax.broadcasted_iota(jnp.int32, (C, C), 0) - lax.broadcasted_iota(jnp.int32, (C, C), 1)
    for d in range(sub):
        if d == 0:
            t = qs * k
        else:
            kd = pltpu.roll(k, d, axis=0)
            bd = pltpu.roll(b, d, axis=0)
            t = qs * kd * jnp.exp2(jnp.minimum(b - bd, 0.0))
        red = jnp.sum(t, axis=1, keepdims=True)
        red = jnp.where(row_in_sub >= d, red, 0.0)
        A = A + jnp.where(delta == d, red, 0.0)
    return A


def _head_norm(o):
    return o * lax.rsqrt(jnp.mean(o * o, axis=-1, keepdims=True) + HEAD_NORM_EPS)


def _gla_prompt_kernel(q_ref, k_ref, v_ref, la_ref, cmat_ref, lmask_ref, o_ref, sout_ref, s_ref,
                       *, scale, sub, groups):
    t = pl.program_id(2)
    C = q_ref.shape[0]
    HB, DK, DV = s_ref.shape

    @pl.when(t == 0)
    def _():
        s_ref[...] = jnp.zeros_like(s_ref)

    cm = cmat_ref[...]
    p0, p1, p2 = _split3(la_ref[...] * LOG2_E)
    b_all = _dot(cm, p0) + _dot(cm, p1) + _dot(cm, p2)
    for hh in range(HB):
        kc = slice(hh * DK, (hh + 1) * DK)
        vc = slice(hh * DV, (hh + 1) * DV)
        qs = q_ref[:, kc] * scale
        k = k_ref[:, kc]
        vb = v_ref[:, vc].astype(BF16)
        b = b_all[:, kc]

        A = _intra_chunk_scores(qs, k, b, groups, lmask_ref, sub)
        S = s_ref[hh]
        o = _dot((qs * jnp.exp2(b)).astype(BF16), S.astype(BF16)) + _dot(A.astype(BF16), vb)
        o_ref[:, vc] = _head_norm(o)

        bl = b[C - 1:C, :]
        ku_t = (k * jnp.exp2(bl - b)).T
        d_s = _dot(ku_t.astype(BF16), vb)
        a_t = jnp.exp2(jnp.broadcast_to(bl, (V7X_LANES, DK))).T
        for n in range(DV // V7X_LANES):
            sl = slice(n * V7X_LANES, (n + 1) * V7X_LANES)
            s_ref[hh, :, sl] = a_t * S[:, sl] + d_s[:, sl]

    @pl.when(t == pl.num_programs(2) - 1)
    def _():
        sout_ref[...] = s_ref[...]


def _gla_prompt(h, la, batch, seq, heads, dk, dv, q_off, k_off, v_off):
    C = GLA_CHUNK_ROWS
    HB = GLA_HEADS_PER_STEP
    nt = seq // C
    groups = []
    g = C
    while g > GLA_SUB:
        groups.append(g)
        g //= 2
    cmat = _cumsum_mat(C)
    lmask = _level_masks(C, groups)
    wk, wv = HB * dk, HB * dv
    assert heads % HB == 0 and q_off % wk == 0 and k_off % wk == 0 and v_off % wv == 0
    qb, kb, vb = q_off // wk, k_off // wk, v_off // wv
    return pl.pallas_call(
        functools.partial(_gla_prompt_kernel, scale=float(dk) ** -0.5, sub=GLA_SUB,
                          groups=tuple(groups)),
        grid=(batch, heads // HB, nt),
        in_specs=[
            pl.BlockSpec((C, wk), lambda bi, hi, t: (bi * nt + t, qb + hi)),
            pl.BlockSpec((C, wk), lambda bi, hi, t: (bi * nt + t, kb + hi)),
            pl.BlockSpec((C, wv), lambda bi, hi, t: (bi * nt + t, vb + hi)),
            pl.BlockSpec((C, wk), lambda bi, hi, t: (bi * nt + t, hi)),
            pl.BlockSpec(cmat.shape, lambda bi, hi, t: (0, 0)),
            pl.BlockSpec(lmask.shape, lambda bi, hi, t: (0, 0)),
        ],
        out_specs=[
            pl.BlockSpec((C, wv), lambda bi, hi, t: (bi * nt + t, hi)),
            pl.BlockSpec((None, HB, dk, dv), lambda bi, hi, t: (bi, hi, 0, 0)),
        ],
        out_shape=[
            jax.ShapeDtypeStruct((batch * seq, heads * dv), F32),
            jax.ShapeDtypeStruct((batch, heads, dk, dv), F32),
        ],
        scratch_shapes=[pltpu.VMEM((HB, dk, dv), F32)],
        compiler_params=_cparams(("parallel", "parallel", "arbitrary")),
        name="gla_prompt",
    )(h, h, h, la, cmat, lmask)


def _gla_sample_kernel(q_ref, k_ref, v_ref, la_ref, cmat_ref, sin_ref, *rest, scale, seq):
    o_ref, sout_ref = rest[-2:]
    R, DK = q_ref.shape
    DV = v_ref.shape[1]
    nb = R // seq

    qs = q_ref[...] * scale
    k = k_ref[...]
    vb = v_ref[...].astype(BF16)
    cm = cmat_ref[...]
    p0, p1, p2 = _split3(la_ref[...] * LOG2_E)
    ball = _dot(cm, p0) + _dot(cm, p1) + _dot(cm, p2)
    b = ball[0:R]
    bl = ball[R:2 * R]

    A = _intra_chunk_scores(qs, k, b, (), None, seq)
    o_intra = _dot(A.astype(BF16), vb)
    qo = qs * jnp.exp2(b)
    ku_t = (k * jnp.exp2(bl - b)).T
    lane_seq = lax.broadcasted_iota(jnp.int32, (DK, R), 1) // seq
    for i in range(nb):
        rows = slice(i * seq, (i + 1) * seq)
        S = sin_ref[i]
        o = _dot(qo[rows].astype(BF16), S.astype(BF16)) + o_intra[rows]
        o_ref[rows, :] = _head_norm(o)
        d_s = _dot(jnp.where(lane_seq == i, ku_t, 0.0).astype(BF16), vb)
        a_t = jnp.exp2(jnp.broadcast_to(bl[i * seq:i * seq + 1, :], (V7X_LANES, DK))).T
        for n in range(DV // V7X_LANES):
            sl = slice(n * V7X_LANES, (n + 1) * V7X_LANES)
            sout_ref[i, :, sl] = a_t * S[:, sl] + d_s[:, sl]


def _gla_sample(h, la, state, new_state, layer, row0, batch, seq, heads, dk, dv, q_off, k_off, v_off):
    R = V7X_LANES
    nb = R // seq
    ng = batch // nb
    rb = row0 // R
    cmat = _seq_cumsum_mats(R, seq)
    qb, kb, vb = q_off // dk, k_off // dk, v_off // dv
    carried = [] if new_state is None else [new_state]
    return pl.pallas_call(
        functools.partial(_gla_sample_kernel, scale=float(dk) ** -0.5, seq=seq),
        grid=(ng, heads),
        in_specs=[
            pl.BlockSpec((R, dk), lambda g, hi: (rb + g, qb + hi)),
            pl.BlockSpec((R, dk), lambda g, hi: (rb + g, kb + hi)),
            pl.BlockSpec((R, dv), lambda g, hi: (rb + g, vb + hi)),
            pl.BlockSpec((R, dk), lambda g, hi: (rb + g, hi)),
            pl.BlockSpec(cmat.shape, lambda g, hi: (0, 0)),
            pl.BlockSpec((None, nb, None, dk, dv), lambda g, hi: (layer, g, hi, 0, 0)),
        ] + [pl.BlockSpec(memory_space=pl.ANY)] * len(carried),
        out_specs=[
            pl.BlockSpec((R, dv), lambda g, hi: (g, hi)),
            pl.BlockSpec((None, nb, None, dk, dv), lambda g, hi: (layer, g, hi, 0, 0)),
        ],
        out_shape=[
            jax.ShapeDtypeStruct((batch * seq, heads * dv), F32),
            jax.ShapeDtypeStruct(state.shape, F32),
        ],
        input_output_aliases={6: 1} if carried else {},
        compiler_params=_cparams(("parallel", "parallel")),
        name="gla_sample",
    )(h, h, h, la, cmat, state, *carried)


def _window_means_minus_input(ext, u, pos, lead):
    n, pw = u.shape
    gw = pw // len(POOL_WINDOWS)
    outs = []
    for g, w in enumerate(POOL_WINDOWS):
        cols = slice(g * gw, (g + 1) * gw)
        p = ext[:, cols]
        span = 1
        while span < w:
            p = p + pltpu.roll(p, span, axis=0)
            span *= 2
        cnt = jnp.minimum(w, pos + 1).astype(F32)
        outs.append(p[lead:lead + n] / cnt - u[:, cols])
    return outs


def _mix_tail(pooled, on, r, ga, gb, x, hg_ref, wgo_ref, pw_ref, ps_ref, wo_ref, g_ref, b_ref, alpha):
    ya = jnp.concatenate(
        [_dot(p.astype(BF16), pw_ref[g]) for g, p in enumerate(pooled)], axis=1) * ps_ref[...]
    yb = _dot((on * hg_ref[...] * (r * jax.nn.sigmoid(r))).astype(BF16), wgo_ref[...])
    merged = jax.nn.sigmoid(ga) * ya + jax.nn.sigmoid(gb) * yb
    m = _dot(merged.astype(BF16), wo_ref[...])
    return _layer_norm(alpha * x + m, g_ref[...], b_ref[...])


def _post_prompt_kernel(on_ref, r_ref, ga_ref, gb_ref, u_ref, x_ref, hg_ref, wgo_ref, pw_ref, ps_ref,
                        wo_ref, g_ref, b_ref, o_ref, npool_ref, halo_ref, *, tiles_per_seq, alpha):
    i = pl.program_id(0)
    tm = u_ref.shape[0]
    lead = halo_ref.shape[0]
    ti = i % tiles_per_seq

    @pl.when(ti == 0)
    def _():
        halo_ref[...] = jnp.zeros_like(halo_ref)

    u = u_ref[...]
    ext = jnp.concatenate([halo_ref[...], u], axis=0)
    halo_ref[...] = u[tm - lead:, :]

    @pl.when(ti == tiles_per_seq - 1)
    def _():
        npool_ref[...] = u[tm - POOL_BUF:, :]

    pos = ti * tm + lax.broadcasted_iota(jnp.int32, (tm, 1), 0)
    pooled = _window_means_minus_input(ext, u, pos, lead)
    o_ref[...] = _mix_tail(pooled, on_ref[...], r_ref[...], ga_ref[...], gb_ref[...], x_ref[...],
                           hg_ref, wgo_ref, pw_ref, ps_ref, wo_ref, g_ref, b_ref, alpha)


def _post_sample_kernel(on_ref, r_ref, ga_ref, gb_ref, u_ref, x_ref, st_ref, hg_ref, wgo_ref, pw_ref,
                        ps_ref, wo_ref, g_ref, b_ref, carried_ref, o_ref, npool_ref, ext_ref,
                        *, seq, start, alpha):
    del carried_ref
    tm, pw = u_ref.shape
    nb = tm // seq
    lead = ext_ref.shape[1] - seq
    u = u_ref[...]
    u3 = u.reshape(nb, seq, pw)
    st = st_ref[...]
    ext_ref[:, 0:lead - POOL_BUF, :] = jnp.zeros((nb, lead - POOL_BUF, pw), F32)
    ext_ref[:, lead - POOL_BUF:lead, :] = st
    ext_ref[:, lead:, :] = u3
    e3 = ext_ref[...]
    npool_ref[...] = e3[:, lead + seq - POOL_BUF:, :]
    p = e3.reshape(nb * (lead + seq), pw)
    gw = pw // len(POOL_WINDOWS)
    pos = start + lax.broadcasted_iota(jnp.int32, (nb, seq, 1), 1)
    pooled = []
    for g, w in enumerate(POOL_WINDOWS):
        cols = slice(g * gw, (g + 1) * gw)
        pg = p[:, cols]
        span = 1
        while span < w:
            pg = pg + pltpu.roll(pg, span, axis=0)
            span *= 2
        win = pg.reshape(nb, lead + seq, gw)[:, lead:, :]
        cnt = jnp.minimum(w, pos + 1).astype(F32)
        pooled.append((win / cnt - u3[:, :, cols]).reshape(tm, gw))
    o_ref[...] = _mix_tail(pooled, on_ref[...], r_ref[...], ga_ref[...], gb_ref[...], x_ref[...],
                           hg_ref, wgo_ref, pw_ref, ps_ref, wo_ref, g_ref, b_ref, alpha)


def _const_spec(shape, index_map):
    return pl.BlockSpec(shape, index_map, pipeline_mode=pl.Buffered(1))


def _post_weight_specs(layer, ln_idx, D, GV, groups, gw, ogw, n):
    z = (0,) * (n - 1)
    return [
        _const_spec((None, 1, GV), lambda *a: (layer, 0, 0)),
        _const_spec((None, GV, D), lambda *a: (layer, 0, 0)),
        _const_spec((None, groups, gw, ogw), lambda *a: (layer, 0, 0, 0)),
        _const_spec((None, 1, D), lambda *a: (layer, 0, 0)),
        _const_spec((None, D, D), lambda *a: (layer, 0, 0)),
        _const_spec((None, None, 1, D), lambda *a: (layer, ln_idx, 0, 0)),
        _const_spec((None, None, 1, D), lambda *a: (layer, ln_idx, 0, 0)),
    ]


def _post_prompt(on, h, x, weights, layer, ln_idx, alpha, seq, tm, cols):
    TP, GV = on.shape
    T, D = x.shape
    hg, wgo, pw, ps, wo, ln_g, ln_b = weights
    groups, gw, ogw = pw.shape[1:]
    PW = groups * gw
    tiles_per_seq = seq // tm
    r_b, ga_b, gb_b, u_b = cols["r"] // GV, cols["ga"] // D, cols["gb"] // D, cols["u"] // PW
    lead = 2 * V7X_SUBLANES
    return pl.pallas_call(
        functools.partial(_post_prompt_kernel, tiles_per_seq=tiles_per_seq, alpha=alpha),
        grid=(TP // tm,),
        in_specs=[
            pl.BlockSpec((tm, GV), lambda i: (i, 0)),
            pl.BlockSpec((tm, GV), lambda i: (i, r_b)),
            pl.BlockSpec((tm, D), lambda i: (i, ga_b)),
            pl.BlockSpec((tm, D), lambda i: (i, gb_b)),
            pl.BlockSpec((tm, PW), lambda i: (i, u_b)),
            pl.BlockSpec((tm, D), lambda i: (i, 0)),
        ] + _post_weight_specs(layer, ln_idx, D, GV, groups, gw, ogw, 1),
        out_specs=[
            pl.BlockSpec((tm, D), lambda i: (i, 0)),
            pl.BlockSpec((None, POOL_BUF, PW), lambda i: (i // tiles_per_seq, 0, 0)),
        ],
        out_shape=[
            jax.ShapeDtypeStruct((T, D), F32),
            jax.ShapeDtypeStruct((TP // seq, POOL_BUF, PW), F32),
        ],
        scratch_shapes=[pltpu.VMEM((lead, PW), F32)],
        compiler_params=_cparams(("arbitrary",)),
        name="post_prompt",
    )(on, h, h, h, h, x, hg, wgo, pw, ps, wo, ln_g, ln_b)


def _post_sample(on, h, x, x2, state_pool, weights, layer, ln_idx, alpha, row0, seq, tm, cols):
    TS, GV = on.shape
    D = x.shape[1]
    hg, wgo, pw, ps, wo, ln_g, ln_b = weights
    groups, gw, ogw = pw.shape[1:]
    PW = groups * gw
    nb = tm // seq
    batch = TS // seq
    rb = row0 // tm
    r_b, ga_b, gb_b, u_b = cols["r"] // GV, cols["ga"] // D, cols["gb"] // D, cols["u"] // PW
    lead = 2 * V7X_SUBLANES
    return pl.pallas_call(
        functools.partial(_post_sample_kernel, seq=seq, start=PAST_LEN, alpha=alpha),
        grid=(TS // tm,),
        in_specs=[
            pl.BlockSpec((tm, GV), lambda i: (i, 0)),
            pl.BlockSpec((tm, GV), lambda i: (rb + i, r_b)),
            pl.BlockSpec((tm, D), lambda i: (rb + i, ga_b)),
            pl.BlockSpec((tm, D), lambda i: (rb + i, gb_b)),
            pl.BlockSpec((tm, PW), lambda i: (rb + i, u_b)),
            pl.BlockSpec((tm, D), lambda i: (rb + i, 0)),
            pl.BlockSpec((None, nb, POOL_BUF, PW), lambda i: (layer, i, 0, 0)),
        ] + _post_weight_specs(layer, ln_idx, D, GV, groups, gw, ogw, 1) + [
            pl.BlockSpec(memory_space=pl.ANY),
        ],
        out_specs=[
            pl.BlockSpec((tm, D), lambda i: (rb + i, 0)),
            pl.BlockSpec((nb, POOL_BUF, PW), lambda i: (i, 0, 0)),
        ],
        out_shape=[
            jax.ShapeDtypeStruct(x2.shape, F32),
            jax.ShapeDtypeStruct((batch, POOL_BUF, PW), F32),
        ],
        input_output_aliases={14: 0},
        scratch_shapes=[pltpu.VMEM((nb, lead + seq, PW), F32)],
        compiler_params=_cparams(("arbitrary",)),
        name="post_sample",
    )(on, h, h, h, h, x, state_pool, hg, wgo, pw, ps, wo, ln_g, ln_b, x2)


def _tiles(T, TP, seq_p, F, NH):
    def fit(n, t):
        while n % t:
            t //= 2
        return t
    return dict(
        ffn_tm=fit(np.gcd(T, TP), 512), ffn_tf=fit(F, 512),
        proj_tm=fit(np.gcd(T, TP), 1024), proj_tn=fit(NH, 1024),
        post_tm=fit(seq_p, 256), post_sample_tm=fit(np.gcd(T - TP, TP), 128),
    )


def kernel(x_prompt, x_sample, state_pool, state_gla, ln_g, ln_b, w_ffn_in, w_ffn_out, w_in, pool_w,
           pool_scale, a_up, a_bias, head_g, w_gla_out, w_out):
    B, SEQ, D = x_prompt.shape
    DB, DSEQ, _ = x_sample.shape
    L = w_in.shape[0]
    PW = state_pool.shape[-1]
    H, DK, DV = state_gla.shape[2:]
    GK, GV = H * DK, H * DV
    RANK = a_up.shape[1]
    F = w_ffn_out.shape[2]
    TP, TS = B * SEQ, DB * DSEQ
    T = TP + TS
    alpha = float((2 * L) ** 0.25)
    assert SEQ % GLA_CHUNK_ROWS == 0 and TP % V7X_LANES == 0 and V7X_LANES % DSEQ == 0
    assert DSEQ == V7X_SUBLANES and PAST_LEN >= POOL_BUF

    lo0 = PW + 2 * GK + GV
    lo1 = lo0 + RANK
    wm = jnp.concatenate([w_in[:, :, lo1:], w_in[:, :, PW + 2 * GK:lo0], w_in[:, :, :PW + 2 * GK]],
                         axis=-1).astype(BF16)
    wa = jnp.pad(w_in[:, :, lo0:lo1], ((0, 0), (0, 0), (0, V7X_LANES - RANK))).astype(BF16)
    aup = jnp.pad(a_up, ((0, 0), (0, V7X_LANES - RANK), (0, 0))).astype(BF16)
    abias = a_bias.reshape(L, 1, GK)
    wf_in = w_ffn_in.astype(BF16)
    wf_out = w_ffn_out.astype(BF16)
    ln_g4 = ln_g.reshape(L, 3, 1, D)
    ln_b4 = ln_b.reshape(L, 3, 1, D)
    post_w = (head_g.reshape(L, 1, GV), w_gla_out.astype(BF16), pool_w.astype(BF16),
              pool_scale.reshape(L, 1, D), w_out.astype(BF16), ln_g4, ln_b4)
    c_v = GV + 2 * D
    c_u = c_v + GV
    cols = dict(r=0, ga=GV, gb=GV + D, v=c_v, u=c_u, q=c_u + PW, k=c_u + PW + GK)
    ts = _tiles(T, TP, SEQ, F, wm.shape[2])
    assert GV == D and c_v % DV == 0 and c_u % PW == 0 and cols["q"] % DK == 0

    ffn = functools.partial(_ffn_ln, w_in_b=wf_in, w_out_b=wf_out, ln_g=ln_g4, ln_b=ln_b4, alpha=alpha,
                            tm=ts["ffn_tm"], tf=ts["ffn_tf"])
    xs = [x_prompt.reshape(TP, D), x_sample.reshape(TS, D)]
    pool_p, gla_p, pool_s, gla_s = [], [], [], None
    for l in range(L):
        (x,) = ffn(xs, [T], layer=l, which=0, ln_idx=0)
        h, la = _proj(x, wm, wa, aup, abias, l, ts["proj_tm"], ts["proj_tn"])
        on_p, s_p = _gla_prompt(h, la, B, SEQ, H, DK, DV, cols["q"], cols["k"], cols["v"])
        on_s, gla_s = _gla_sample(h, la, state_gla, gla_s, l, TP, DB, DSEQ, H, DK, DV,
                                  cols["q"], cols["k"], cols["v"])
        x2, np_p = _post_prompt(on_p, h, x, post_w, l, 1, alpha, SEQ, ts["post_tm"], cols)
        x2, np_s = _post_sample(on_s, h, x, x2, state_pool, post_w, l, 1, alpha, TP, DSEQ,
                                ts["post_sample_tm"], cols)
        xs = ffn([x2], [T] if l + 1 < L else [TP, TS], layer=l, which=1, ln_idx=2)
        pool_p.append(np_p)
        gla_p.append(s_p)
        pool_s.append(np_s)
    return (xs[0].reshape(B, SEQ, D), xs[1].reshape(DB, DSEQ, D), jnp.stack(pool_p),
            jnp.stack(gla_p), jnp.stack(pool_s), gla_s)
```

```python
import functools

import numpy as np
import jax
import jax.numpy as jnp
from jax import lax
from jax.experimental import pallas as pl
from jax.experimental.pallas import tpu as pltpu

POOL_WINDOWS = (2, 4, 8, 16)
POOL_BUF = max(POOL_WINDOWS) - 1
GLA_TAU = 16.0
LN_EPS = 1e-5
HEAD_NORM_EPS = 1e-6
PAST_LEN = 16384

V7X_LANES = 128
V7X_SUBLANES = 8
V7X_VMEM_LIMIT_BYTES = 56 * 1024 * 1024

F32 = jnp.float32
BF16 = jnp.bfloat16

GLA_SUB = 4
GLA_CHUNK_ROWS = 128
GLA_HEADS_PER_STEP = 4
LOG2_E = 1.4426950408889634


def _cparams(sem):
    return pltpu.CompilerParams(dimension_semantics=sem, vmem_limit_bytes=V7X_VMEM_LIMIT_BYTES)


def _layer_norm(y, g, b):
    mu = jnp.mean(y, axis=-1, keepdims=True)
    yc = y - mu
    var = jnp.mean(yc * yc, axis=-1, keepdims=True)
    return yc * lax.rsqrt(var + LN_EPS) * g + b


def _dot(a, b):
    return jnp.dot(a, b, preferred_element_type=F32)


def _dot_nt(a, b):
    return lax.dot_general(a, b, (((1,), (1,)), ((), ())), preferred_element_type=F32)


def _split3(x):
    p0 = x.astype(BF16)
    r1 = x - p0.astype(F32)
    p1 = r1.astype(BF16)
    p2 = (r1 - p1.astype(F32)).astype(BF16)
    return p0, p1, p2


def _ffn_ln_kernel(*refs, alpha, n_in, n_out, n_p, tile0, n_carried, cast_weights):
    refs = list(refs)
    x_refs = [refs.pop(0) for _ in range(n_in)]
    wg_ref, wu_ref, wo_ref, g_ref, b_ref = [refs.pop(0) for _ in range(5)]
    del refs[:n_carried]
    o_refs = [refs.pop(0) for _ in range(n_out)]
    wb_refs = [refs.pop(0) for _ in range(3)] if cast_weights else None
    xb_ref, acc_ref = refs
    i = pl.program_id(0) + tile0
    j = pl.program_id(1)
    in_prompt = i < n_p

    def per_group(fn):
        if n_in == 1 and n_out == 1:
            fn(x_refs[0], o_refs[0])
        else:
            pl.when(in_prompt)(lambda: fn(x_refs[0], o_refs[0]))
            pl.when(jnp.logical_not(in_prompt))(lambda: fn(x_refs[-1], o_refs[-1]))

    @pl.when(j == 0)
    def _():
        def load(x_ref, _):
            xb_ref[...] = x_ref[...].astype(BF16)
        per_group(load)
        acc_ref[...] = jnp.zeros_like(acc_ref)

    wg, wu, wo = wg_ref[...], wu_ref[...], wo_ref[...]
    if cast_weights:
        wg, wu, wo = wg.astype(BF16), wu.astype(BF16), wo.astype(BF16)
        for w_ref, w in zip(wb_refs, (wg, wu, wo)):
            w_ref[...] = w
    xb = xb_ref[...]
    gate = _dot(xb, wg)
    up = _dot(xb, wu)
    act = (gate * jax.nn.sigmoid(gate) * up).astype(BF16)
    acc_ref[...] += _dot(act, wo)

    @pl.when(j == pl.num_programs(1) - 1)
    def _():
        def finish(x_ref, o_ref):
            y = alpha * x_ref[...] + 0.5 * acc_ref[...]
            o_ref[...] = _layer_norm(y, g_ref[...], b_ref[...])
        per_group(finish)


def _ffn_ln(xs, out_rows, w_ffn_in, w_ffn_out, ln_g, ln_b, layer, which, ln_idx, alpha, tm, tf):
    D = xs[0].shape[1]
    T = sum(x.shape[0] for x in xs)
    F = w_ffn_out.shape[2]
    n_p = (xs[0].shape[0] if len(xs) == 2 else out_rows[0]) // tm
    ln_specs = [pl.BlockSpec((None, None, 1, D), lambda i, j: (layer, ln_idx, 0, 0))] * 2
    scratch = [pltpu.VMEM((tm, D), BF16), pltpu.VMEM((tm, D), F32)]
    kern = functools.partial(_ffn_ln_kernel, alpha=alpha, n_p=n_p)

    tf0 = tf // 2
    nf0 = F // tf0
    first = pl.pallas_call(
        functools.partial(kern, n_in=1, n_out=1, tile0=0, n_carried=0, cast_weights=True),
        grid=(1, nf0),
        in_specs=[
            pl.BlockSpec((tm, D), lambda i, j: (0, 0)),
            pl.BlockSpec((None, None, D, tf0), lambda i, j: (layer, which, 0, j)),
            pl.BlockSpec((None, None, D, tf0), lambda i, j: (layer, which, 0, j + nf0)),
            pl.BlockSpec((None, None, tf0, D), lambda i, j: (layer, which, j, 0)),
        ] + ln_specs,
        out_specs=[
            pl.BlockSpec((tm, D), lambda i, j: (0, 0)),
            pl.BlockSpec((D, tf0), lambda i, j: (0, j)),
            pl.BlockSpec((D, tf0), lambda i, j: (0, j)),
            pl.BlockSpec((tf0, D), lambda i, j: (j, 0)),
        ],
        out_shape=[
            jax.ShapeDtypeStruct((out_rows[0], D), F32),
            jax.ShapeDtypeStruct((D, F), BF16),
            jax.ShapeDtypeStruct((D, F), BF16),
            jax.ShapeDtypeStruct((F, D), BF16),
        ],
        scratch_shapes=scratch,
        compiler_params=_cparams(("arbitrary", "arbitrary")),
        name="ffn_ln_first",
    )
    y0, wg_b, wu_b, wo_b = first(xs[0], w_ffn_in, w_ffn_in, w_ffn_out, ln_g, ln_b)

    def row_specs(n):
        if n == 1:
            return [pl.BlockSpec((tm, D), lambda i, j: (i + 1, 0))]
        return [pl.BlockSpec((tm, D), lambda i, j: (jnp.minimum(i + 1, n_p - 1), 0)),
                pl.BlockSpec((tm, D), lambda i, j: (jnp.maximum(i + 1 - n_p, 0), 0))]

    nf = F // tf
    n_fixed = len(xs) + 5
    outs = pl.pallas_call(
        functools.partial(kern, n_in=len(xs), n_out=len(out_rows), tile0=1, n_carried=1,
                          cast_weights=False),
        grid=(T // tm - 1, nf),
        in_specs=row_specs(len(xs)) + [
            pl.BlockSpec((D, tf), lambda i, j: (0, j)),
            pl.BlockSpec((D, tf), lambda i, j: (0, j)),
            pl.BlockSpec((tf, D), lambda i, j: (j, 0)),
        ] + ln_specs + [pl.BlockSpec(memory_space=pl.ANY)],
        out_specs=row_specs(len(out_rows)),
        out_shape=[jax.ShapeDtypeStruct((r, D), F32) for r in out_rows],
        input_output_aliases={n_fixed: 0},
        scratch_shapes=scratch,
        compiler_params=_cparams(("arbitrary", "arbitrary")),
        name="ffn_ln",
    )(*xs, wg_b, wu_b, wo_b, ln_g, ln_b, y0)
    return list(outs)


def _proj_kernel(x_ref, wm_ref, wa_ref, aup_ref, ab_ref, h_ref, la_ref, xb_ref):
    j = pl.program_id(1)

    @pl.when(j == 0)
    def _():
        xb = x_ref[...].astype(BF16)
        xb_ref[...] = xb
        a_lo = _dot(xb, wa_ref[...])
        z = _dot(a_lo.astype(BF16), aup_ref[...]) + ab_ref[...]
        la_ref[...] = (jnp.minimum(z, 0.0) - jnp.log(1.0 + jnp.exp(-jnp.abs(z)))) / GLA_TAU

    h_ref[...] = _dot(xb_ref[...], wm_ref[...])


def _proj(x, wm, wa, aup, abias, layer, tm, tn):
    T, D = x.shape
    NH = wm.shape[2]
    GK = aup.shape[2]
    return pl.pallas_call(
        _proj_kernel,
        grid=(T // tm, NH // tn),
        in_specs=[
            pl.BlockSpec((tm, D), lambda i, j: (i, 0)),
            pl.BlockSpec((None, D, tn), lambda i, j: (layer, 0, j)),
            pl.BlockSpec((None, D, V7X_LANES), lambda i, j: (layer, 0, 0)),
            pl.BlockSpec((None, V7X_LANES, GK), lambda i, j: (layer, 0, 0)),
            pl.BlockSpec((None, 1, GK), lambda i, j: (layer, 0, 0)),
        ],
        out_specs=[
            pl.BlockSpec((tm, tn), lambda i, j: (i, j)),
            pl.BlockSpec((tm, GK), lambda i, j: (i, 0)),
        ],
        out_shape=[jax.ShapeDtypeStruct((T, NH), F32), jax.ShapeDtypeStruct((T, GK), F32)],
        scratch_shapes=[pltpu.VMEM((tm, D), BF16)],
        compiler_params=_cparams(("parallel", "arbitrary")),
        name="proj",
    )(x, wm, wa, aup, abias)


def _stack01(mats, dtype):
    return jnp.asarray(np.concatenate(mats, axis=0).astype(np.float32), dtype=dtype)


def _cumsum_mat(rows):
    i = np.arange(rows)[:, None]
    j = np.arange(rows)[None, :]
    return _stack01([j <= i], BF16)


def _seq_cumsum_mats(rows, seq):
    i = np.arange(rows)[:, None]
    j = np.arange(rows)[None, :]
    same = i // seq == j // seq
    return _stack01([same & (j <= i), same], BF16)


def _level_masks(rows, groups):
    i = np.arange(rows)[:, None]
    j = np.arange(rows)[None, :]
    out = []
    for grp in groups:
        half = grp // 2
        out.append((i // grp == j // grp) & (i % grp >= half) & (j % grp < half))
    return _stack01(out, F32)


def _intra_chunk_scores(qs, k, b, groups, lmask_ref, sub):
    C = qs.shape[0]
    A = jnp.zeros((C, C), F32)
    for n, grp in enumerate(groups):
        e = jnp.concatenate(
            [jnp.exp2(-jnp.abs(b[s:s + grp] - b[s + grp // 2 - 1:s + grp // 2])) for s in range(0, C, grp)],
            axis=0)
        A = A + _dot_nt((qs * e).astype(BF16), (k * e).astype(BF16)) * lmask_ref[n * C:(n + 1) * C, :]
    row_in_sub = lax.broadcasted_iota(jnp.int32, (C, 1), 0) & (sub - 1)
    delta = lax.broadcasted_iota(jnp.int32, (C, C), 0) - lax.broadcasted_iota(jnp.int32, (C, C), 1)
    for d in range(sub):
        if d == 0:
            t = qs * k
        else:
            kd = pltpu.roll(k, d, axis=0)
            bd = pltpu.roll(b, d, axis=0)
            t = qs * kd * jnp.exp2(jnp.minimum(b - bd, 0.0))
        red = jnp.sum(t, axis=1, keepdims=True)
        red = jnp.where(row_in_sub >= d, red, 0.0)
        A = A + jnp.where(delta == d, red, 0.0)
    return A


def _head_norm(o):
    return o * lax.rsqrt(jnp.mean(o * o, axis=-1, keepdims=True) + HEAD_NORM_EPS)


def _gla_prompt_kernel(q_ref, k_ref, v_ref, la_ref, cmat_ref, lmask_ref, o_ref, sout_ref, s_ref,
                       *, scale, sub, groups):
    t = pl.program_id(2)
    C = q_ref.shape[0]
    HB, DK, DV = s_ref.shape

    @pl.when(t == 0)
    def _():
        s_ref[...] = jnp.zeros_like(s_ref)

    cm = cmat_ref[...]
    p0, p1, p2 = _split3(la_ref[...] * LOG2_E)
    b_all = _dot(cm, p0) + _dot(cm, p1) + _dot(cm, p2)
    for hh in range(HB):
        kc = slice(hh * DK, (hh + 1) * DK)
        vc = slice(hh * DV, (hh + 1) * DV)
        qs = q_ref[:, kc] * scale
        k = k_ref[:, kc]
        vb = v_ref[:, vc].astype(BF16)
        b = b_all[:, kc]

        A = _intra_chunk_scores(qs, k, b, groups, lmask_ref, sub)
        S = s_ref[hh]
        o = _dot((qs * jnp.exp2(b)).astype(BF16), S.astype(BF16)) + _dot(A.astype(BF16), vb)
        o_ref[:, vc] = _head_norm(o)

        bl = b[C - 1:C, :]
        ku_t = (k * jnp.exp2(bl - b)).T
        d_s = _dot(ku_t.astype(BF16), vb)
        a_t = jnp.exp2(jnp.broadcast_to(bl, (V7X_LANES, DK))).T
        for n in range(DV // V7X_LANES):
            sl = slice(n * V7X_LANES, (n + 1) * V7X_LANES)
            s_ref[hh, :, sl] = a_t * S[:, sl] + d_s[:, sl]

    @pl.when(t == pl.num_programs(2) - 1)
    def _():
        sout_ref[...] = s_ref[...]


def _gla_prompt(h, la, batch, seq, heads, dk, dv, q_off, k_off, v_off):
    C = GLA_CHUNK_ROWS
    HB = GLA_HEADS_PER_STEP
    nt = seq // C
    groups = []
    g = C
    while g > GLA_SUB:
        groups.append(g)
        g //= 2
    cmat = _cumsum_mat(C)
    lmask = _level_masks(C, groups)
    wk, wv = HB * dk, HB * dv
    assert heads % HB == 0 and q_off % wk == 0 and k_off % wk == 0 and v_off % wv == 0
    qb, kb, vb = q_off // wk, k_off // wk, v_off // wv
    return pl.pallas_call(
        functools.partial(_gla_prompt_kernel, scale=float(dk) ** -0.5, sub=GLA_SUB,
                          groups=tuple(groups)),
        grid=(batch, heads // HB, nt),
        in_specs=[
            pl.BlockSpec((C, wk), lambda bi, hi, t: (bi * nt + t, qb + hi)),
            pl.BlockSpec((C, wk), lambda bi, hi, t: (bi * nt + t, kb + hi)),
            pl.BlockSpec((C, wv), lambda bi, hi, t: (bi * nt + t, vb + hi)),
            pl.BlockSpec((C, wk), lambda bi, hi, t: (bi * nt + t, hi)),
            pl.BlockSpec(cmat.shape, lambda bi, hi, t: (0, 0)),
            pl.BlockSpec(lmask.shape, lambda bi, hi, t: (0, 0)),
        ],
        out_specs=[
            pl.BlockSpec((C, wv), lambda bi, hi, t: (bi * nt + t, hi)),
            pl.BlockSpec((None, HB, dk, dv), lambda bi, hi, t: (bi, hi, 0, 0)),
        ],
        out_shape=[
            jax.ShapeDtypeStruct((batch * seq, heads * dv), F32),
            jax.ShapeDtypeStruct((batch, heads, dk, dv), F32),
        ],
        scratch_shapes=[pltpu.VMEM((HB, dk, dv), F32)],
        compiler_params=_cparams(("parallel", "parallel", "arbitrary")),
        name="gla_prompt",
    )(h, h, h, la, cmat, lmask)


def _gla_sample_kernel(q_ref, k_ref, v_ref, la_ref, cmat_ref, sin_ref, *rest, scale, seq):
    o_ref, sout_ref = rest[-2:]
    R, DK = q_ref.shape
    DV = v_ref.shape[1]
    nb = R // seq

    qs = q_ref[...] * scale
    k = k_ref[...]
    vb = v_ref[...].astype(BF16)
    cm = cmat_ref[...]
    p0, p1, p2 = _split3(la_ref[...] * LOG2_E)
    ball = _dot(cm, p0) + _dot(cm, p1) + _dot(cm, p2)
    b = ball[0:R]
    bl = ball[R:2 * R]

    A = _intra_chunk_scores(qs, k, b, (), None, seq)
    o_intra = _dot(A.astype(BF16), vb)
    qo = qs * jnp.exp2(b)
    ku_t = (k * jnp.exp2(bl - b)).T
    lane_seq = lax.broadcasted_iota(jnp.int32, (DK, R), 1) // seq
    for i in range(nb):
        rows = slice(i * seq, (i + 1) * seq)
        S = sin_ref[i]
        o = _dot(qo[rows].astype(BF16), S.astype(BF16)) + o_intra[rows]
        o_ref[rows, :] = _head_norm(o)
        d_s = _dot(jnp.where(lane_seq == i, ku_t, 0.0).astype(BF16), vb)
        a_t = jnp.exp2(jnp.broadcast_to(bl[i * seq:i * seq + 1, :], (V7X_LANES, DK))).T
        for n in range(DV // V7X_LANES):
            sl = slice(n * V7X_LANES, (n + 1) * V7X_LANES)
            sout_ref[i, :, sl] = a_t * S[:, sl] + d_s[:, sl]


def _gla_sample(h, la, state, new_state, layer, row0, batch, seq, heads, dk, dv, q_off, k_off, v_off):
    R = V7X_LANES
    nb = R // seq
    ng = batch // nb
    rb = row0 // R
    cmat = _seq_cumsum_mats(R, seq)
    qb, kb, vb = q_off // dk, k_off // dk, v_off // dv
    carried = [] if new_state is None else [new_state]
    return pl.pallas_call(
        functools.partial(_gla_sample_kernel, scale=float(dk) ** -0.5, seq=seq),
        grid=(ng, heads),
        in_specs=[
            pl.BlockSpec((R, dk), lambda g, hi: (rb + g, qb + hi)),
            pl.BlockSpec((R, dk), lambda g, hi: (rb + g, kb + hi)),
            pl.BlockSpec((R, dv), lambda g, hi: (rb + g, vb + hi)),
            pl.BlockSpec((R, dk), lambda g, hi: (rb + g, hi)),
            pl.BlockSpec(cmat.shape, lambda g, hi: (0, 0)),
            pl.BlockSpec((None, nb, None, dk, dv), lambda g, hi: (layer, g, hi, 0, 0)),
        ] + [pl.BlockSpec(memory_space=pl.ANY)] * len(carried),
        out_specs=[
            pl.BlockSpec((R, dv), lambda g, hi: (g, hi)),
            pl.BlockSpec((None, nb, None, dk, dv), lambda g, hi: (layer, g, hi, 0, 0)),
        ],
        out_shape=[
            jax.ShapeDtypeStruct((batch * seq, heads * dv), F32),
            jax.ShapeDtypeStruct(state.shape, F32),
        ],
        input_output_aliases={6: 1} if carried else {},
        compiler_params=_cparams(("parallel", "parallel")),
        name="gla_sample",
    )(h, h, h, la, cmat, state, *carried)


def _window_means_minus_input(ext, u, pos, lead):
    n, pw = u.shape
    gw = pw // len(POOL_WINDOWS)
    outs = []
    for g, w in enumerate(POOL_WINDOWS):
        cols = slice(g * gw, (g + 1) * gw)
        p = ext[:, cols]
        span = 1
        while span < w:
            p = p + pltpu.roll(p, span, axis=0)
            span *= 2
        cnt = jnp.minimum(w, pos + 1).astype(F32)
        outs.append(p[lead:lead + n] / cnt - u[:, cols])
    return outs


def _mix_tail(pooled, on, r, ga, gb, x, hg_ref, wgo_ref, pw_ref, ps_ref, wo_ref, g_ref, b_ref, alpha):
    ya = jnp.concatenate(
        [_dot(p.astype(BF16), pw_ref[g]) for g, p in enumerate(pooled)], axis=1) * ps_ref[...]
    yb = _dot((on * hg_ref[...] * (r * jax.nn.sigmoid(r))).astype(BF16), wgo_ref[...])
    merged = jax.nn.sigmoid(ga) * ya + jax.nn.sigmoid(gb) * yb
    m = _dot(merged.astype(BF16), wo_ref[...])
    return _layer_norm(alpha * x + m, g_ref[...], b_ref[...])


def _post_prompt_kernel(on_ref, r_ref, ga_ref, gb_ref, u_ref, x_ref, hg_ref, wgo_ref, pw_ref, ps_ref,
                        wo_ref, g_ref, b_ref, o_ref, npool_ref, halo_ref, *, tiles_per_seq, alpha):
    i = pl.program_id(0)
    tm = u_ref.shape[0]
    lead = halo_ref.shape[0]
    ti = i % tiles_per_seq

    @pl.when(ti == 0)
    def _():
        halo_ref[...] = jnp.zeros_like(halo_ref)

    u = u_ref[...]
    ext = jnp.concatenate([halo_ref[...], u], axis=0)
    halo_ref[...] = u[tm - lead:, :]

    @pl.when(ti == tiles_per_seq - 1)
    def _():
        npool_ref[...] = u[tm - POOL_BUF:, :]

    pos = ti * tm + lax.broadcasted_iota(jnp.int32, (tm, 1), 0)
    pooled = _window_means_minus_input(ext, u, pos, lead)
    o_ref[...] = _mix_tail(pooled, on_ref[...], r_ref[...], ga_ref[...], gb_ref[...], x_ref[...],
                           hg_ref, wgo_ref, pw_ref, ps_ref, wo_ref, g_ref, b_ref, alpha)


def _post_sample_kernel(on_ref, r_ref, ga_ref, gb_ref, u_ref, x_ref, st_ref, hg_ref, wgo_ref, pw_ref,
                        ps_ref, wo_ref, g_ref, b_ref, carried_ref, o_ref, npool_ref, ext_ref,
                        *, seq, start, alpha):
    del carried_ref
    tm, pw = u_ref.shape
    nb = tm // seq
    lead = ext_ref.shape[1] - seq
    u = u_ref[...]
    u3 = u.reshape(nb, seq, pw)
    st = st_ref[...]
    ext_ref[:, 0:lead - POOL_BUF, :] = jnp.zeros((nb, lead - POOL_BUF, pw), F32)
    ext_ref[:, lead - POOL_BUF:lead, :] = st
    ext_ref[:, lead:, :] = u3
    e3 = ext_ref[...]
    npool_ref[...] = e3[:, lead + seq - POOL_BUF:, :]
    p = e3.reshape(nb * (lead + seq), pw)
    gw = pw // len(POOL_WINDOWS)
    pos = start + lax.broadcasted_iota(jnp.int32, (nb, seq, 1), 1)
    pooled = []
    for g, w in enumerate(POOL_WINDOWS):
        cols = slice(g * gw, (g + 1) * gw)
        pg = p[:, cols]
        span = 1
        while span < w:
            pg = pg + pltpu.roll(pg, span, axis=0)
            span *= 2
        win = pg.reshape(nb, lead + seq, gw)[:, lead:, :]
        cnt = jnp.minimum(w, pos + 1).astype(F32)
        pooled.append((win / cnt - u3[:, :, cols]).reshape(tm, gw))
    o_ref[...] = _mix_tail(pooled, on_ref[...], r_ref[...], ga_ref[...], gb_ref[...], x_ref[...],
                           hg_ref, wgo_ref, pw_ref, ps_ref, wo_ref, g_ref, b_ref, alpha)


def _const_spec(shape, index_map):
    return pl.BlockSpec(shape, index_map, pipeline_mode=pl.Buffered(1))


def _post_weight_specs(layer, ln_idx, D, GV, groups, gw, ogw, n):
    z = (0,) * (n - 1)
    return [
        _const_spec((None, 1, GV), lambda *a: (layer, 0, 0)),
        _const_spec((None, GV, D), lambda *a: (layer, 0, 0)),
        _const_spec((None, groups, gw, ogw), lambda *a: (layer, 0, 0, 0)),
        _const_spec((None, 1, D), lambda *a: (layer, 0, 0)),
        _const_spec((None, D, D), lambda *a: (layer, 0, 0)),
        _const_spec((None, None, 1, D), lambda *a: (layer, ln_idx, 0, 0)),
        _const_spec((None, None, 1, D), lambda *a: (layer, ln_idx, 0, 0)),
    ]


def _post_prompt(on, h, x, weights, layer, ln_idx, alpha, seq, tm, cols):
    TP, GV = on.shape
    T, D = x.shape
    hg, wgo, pw, ps, wo, ln_g, ln_b = weights
    groups, gw, ogw = pw.shape[1:]
    PW = groups * gw
    tiles_per_seq = seq // tm
    r_b, ga_b, gb_b, u_b = cols["r"] // GV, cols["ga"] // D, cols["gb"] // D, cols["u"] // PW
    lead = 2 * V7X_SUBLANES
    return pl.pallas_call(
        functools.partial(_post_prompt_kernel, tiles_per_seq=tiles_per_seq, alpha=alpha),
        grid=(TP // tm,),
        in_specs=[
            pl.BlockSpec((tm, GV), lambda i: (i, 0)),
            pl.BlockSpec((tm, GV), lambda i: (i, r_b)),
            pl.BlockSpec((tm, D), lambda i: (i, ga_b)),
            pl.BlockSpec((tm, D), lambda i: (i, gb_b)),
            pl.BlockSpec((tm, PW), lambda i: (i, u_b)),
            pl.BlockSpec((tm, D), lambda i: (i, 0)),
        ] + _post_weight_specs(layer, ln_idx, D, GV, groups, gw, ogw, 1),
        out_specs=[
            pl.BlockSpec((tm, D), lambda i: (i, 0)),
            pl.BlockSpec((None, POOL_BUF, PW), lambda i: (i // tiles_per_seq, 0, 0)),
        ],
        out_shape=[
            jax.ShapeDtypeStruct((T, D), F32),
            jax.ShapeDtypeStruct((TP // seq, POOL_BUF, PW), F32),
        ],
        scratch_shapes=[pltpu.VMEM((lead, PW), F32)],
        compiler_params=_cparams(("arbitrary",)),
        name="post_prompt",
    )(on, h, h, h, h, x, hg, wgo, pw, ps, wo, ln_g, ln_b)


def _post_sample(on, h, x, x2, state_pool, weights, layer, ln_idx, alpha, row0, seq, tm, cols):
    TS, GV = on.shape
    D = x.shape[1]
    hg, wgo, pw, ps, wo, ln_g, ln_b = weights
    groups, gw, ogw = pw.shape[1:]
    PW = groups * gw
    nb = tm // seq
    batch = TS // seq
    rb = row0 // tm
    r_b, ga_b, gb_b, u_b = cols["r"] // GV, cols["ga"] // D, cols["gb"] // D, cols["u"] // PW
    lead = 2 * V7X_SUBLANES
    return pl.pallas_call(
        functools.partial(_post_sample_kernel, seq=seq, start=PAST_LEN, alpha=alpha),
        grid=(TS // tm,),
        in_specs=[
            pl.BlockSpec((tm, GV), lambda i: (i, 0)),
            pl.BlockSpec((tm, GV), lambda i: (rb + i, r_b)),
            pl.BlockSpec((tm, D), lambda i: (rb + i, ga_b)),
            pl.BlockSpec((tm, D), lambda i: (rb + i, gb_b)),
            pl.BlockSpec((tm, PW), lambda i: (rb + i, u_b)),
            pl.BlockSpec((tm, D), lambda i: (rb + i, 0)),
            pl.BlockSpec((None, nb, POOL_BUF, PW), lambda i: (layer, i, 0, 0)),
        ] + _post_weight_specs(layer, ln_idx, D, GV, groups, gw, ogw, 1) + [
            pl.BlockSpec(memory_space=pl.ANY),
        ],
        out_specs=[
            pl.BlockSpec((tm, D), lambda i: (rb + i, 0)),
            pl.BlockSpec((nb, POOL_BUF, PW), lambda i: (i, 0, 0)),
        ],
        out_shape=[
            jax.ShapeDtypeStruct(x2.shape, F32),
            jax.ShapeDtypeStruct((batch, POOL_BUF, PW), F32),
        ],
        input_output_aliases={14: 0},
        scratch_shapes=[pltpu.VMEM((nb, lead + seq, PW), F32)],
        compiler_params=_cparams(("arbitrary",)),
        name="post_sample",
    )(on, h, h, h, h, x, state_pool, hg, wgo, pw, ps, wo, ln_g, ln_b, x2)


def _tiles(T, TP, seq_p, F, NH):
    def fit(n, t):
        while n % t:
            t //= 2
        return t
    return dict(
        ffn_tm=fit(np.gcd(T, TP), 512), ffn_tf=fit(F, 512),
        proj_tm=fit(np.gcd(T, TP), 1024), proj_tn=fit(NH, 1024),
        post_tm=fit(seq_p, 256), post_sample_tm=fit(np.gcd(T - TP, TP), 128),
    )


def kernel(x_prompt, x_sample, state_pool, state_gla, ln_g, ln_b, w_ffn_in, w_ffn_out, w_in, pool_w,
           pool_scale, a_up, a_bias, head_g, w_gla_out, w_out):
    B, SEQ, D = x_prompt.shape
    DB, DSEQ, _ = x_sample.shape
    L = w_in.shape[0]
    PW = state_pool.shape[-1]
    H, DK, DV = state_gla.shape[2:]
    GK, GV = H * DK, H * DV
    RANK = a_up.shape[1]
    F = w_ffn_out.shape[2]
    TP, TS = B * SEQ, DB * DSEQ
    T = TP + TS
    alpha = float((2 * L) ** 0.25)
    assert SEQ % GLA_CHUNK_ROWS == 0 and TP % V7X_LANES == 0 and V7X_LANES % DSEQ == 0
    assert DSEQ == V7X_SUBLANES and PAST_LEN >= POOL_BUF

    lo0 = PW + 2 * GK + GV
    lo1 = lo0 + RANK
    w_in_b = w_in.astype(BF16)
    wm = jnp.concatenate([w_in_b[:, :, lo1:], w_in_b[:, :, PW + 2 * GK:lo0], w_in_b[:, :, :PW + 2 * GK]],
                         axis=-1)
    wa = jnp.pad(w_in_b[:, :, lo0:lo1], ((0, 0), (0, 0), (0, V7X_LANES - RANK)))
    aup = jnp.pad(a_up, ((0, 0), (0, V7X_LANES - RANK), (0, 0))).astype(BF16)
    abias = a_bias.reshape(L, 1, GK)
    ln_g4 = ln_g.reshape(L, 3, 1, D)
    ln_b4 = ln_b.reshape(L, 3, 1, D)
    post_w = (head_g.reshape(L, 1, GV), w_gla_out.astype(BF16), pool_w.astype(BF16),
              pool_scale.reshape(L, 1, D), w_out.astype(BF16), ln_g4, ln_b4)
    c_v = GV + 2 * D
    c_u = c_v + GV
    cols = dict(r=0, ga=GV, gb=GV + D, v=c_v, u=c_u, q=c_u + PW, k=c_u + PW + GK)
    ts = _tiles(T, TP, SEQ, F, wm.shape[2])
    assert GV == D and c_v % DV == 0 and c_u % PW == 0 and cols["q"] % DK == 0

    ffn = functools.partial(_ffn_ln, w_ffn_in=w_ffn_in, w_ffn_out=w_ffn_out, ln_g=ln_g4, ln_b=ln_b4,
                            alpha=alpha, tm=ts["ffn_tm"], tf=ts["ffn_tf"])
    xs = [x_prompt.reshape(TP, D), x_sample.reshape(TS, D)]
    pool_p, gla_p, pool_s, gla_s = [], [], [], None
    for l in range(L):
        (x,) = ffn(xs, [T], layer=l, which=0, ln_idx=0)
        h, la = _proj(x, wm, wa, aup, abias, l, ts["proj_tm"], ts["proj_tn"])
        on_p, s_p = _gla_prompt(h, la, B, SEQ, H, DK, DV, cols["q"], cols["k"], cols["v"])
        on_s, gla_s = _gla_sample(h, la, state_gla, gla_s, l, TP, DB, DSEQ, H, DK, DV,
                                  cols["q"], cols["k"], cols["v"])
        x2, np_p = _post_prompt(on_p, h, x, post_w, l, 1, alpha, SEQ, ts["post_tm"], cols)
        x2, np_s = _post_sample(on_s, h, x, x2, state_pool, post_w, l, 1, alpha, TP, DSEQ,
                                ts["post_sample_tm"], cols)
        xs = ffn([x2], [T] if l + 1 < L else [TP, TS], layer=l, which=1, ln_idx=2)
        pool_p.append(np_p)
        gla_p.append(s_p)
        pool_s.append(np_s)
    return (xs[0].reshape(B, SEQ, D), xs[1].reshape(DB, DSEQ, D), jnp.stack(pool_p),
            jnp.stack(gla_p), jnp.stack(pool_s), gla_s)
```

```python
import functools

import numpy as np
import jax
import jax.numpy as jnp
from jax import lax
from jax.experimental import pallas as pl
from jax.experimental.pallas import tpu as pltpu

POOL_WINDOWS = (2, 4, 8, 16)
POOL_BUF = max(POOL_WINDOWS) - 1
GLA_TAU = 16.0
LN_EPS = 1e-5
HEAD_NORM_EPS = 1e-6
PAST_LEN = 16384

V7X_LANES = 128
V7X_SUBLANES = 8
V7X_VMEM_BYTES = 64 * 1024 * 1024
V7X_VMEM_LIMIT_BYTES = V7X_VMEM_BYTES - 1024 * 1024

F32 = jnp.float32
BF16 = jnp.bfloat16

GLA_SUB = 4
GLA_CHUNK_ROWS = 128
GLA_HEADS_PER_STEP = 4
LOG2_E = 1.4426950408889634


def _cparams(sem):
    return pltpu.CompilerParams(dimension_semantics=sem, vmem_limit_bytes=V7X_VMEM_LIMIT_BYTES)


def _layer_norm(y, g, b):
    mu = jnp.mean(y, axis=-1, keepdims=True)
    yc = y - mu
    var = jnp.mean(yc * yc, axis=-1, keepdims=True)
    return yc * lax.rsqrt(var + LN_EPS) * g + b


def _dot(a, b):
    return jnp.dot(a, b, preferred_element_type=F32)


def _dot_nt(a, b):
    return lax.dot_general(a, b, (((1,), (1,)), ((), ())), preferred_element_type=F32)


def _split3(x):
    p0 = x.astype(BF16)
    r1 = x - p0.astype(F32)
    p1 = r1.astype(BF16)
    p2 = (r1 - p1.astype(F32)).astype(BF16)
    return p0, p1, p2


def _ffn_ln_kernel(*refs, alpha, n_in, n_out, n_p, tile0, n_carried, cast_weights):
    refs = list(refs)
    x_refs = [refs.pop(0) for _ in range(n_in)]
    wg_ref, wu_ref, wo_ref, g_ref, b_ref = [refs.pop(0) for _ in range(5)]
    del refs[:n_carried]
    o_refs = [refs.pop(0) for _ in range(n_out)]
    wb_refs = [refs.pop(0) for _ in range(3)] if cast_weights else None
    xb_ref = refs.pop(0)
    acc_ref = o_refs[0] if n_out == 1 else refs.pop(0)
    i = pl.program_id(0) + tile0
    j = pl.program_id(1)
    in_prompt = i < n_p

    def per_group(fn):
        if n_in == 1 and n_out == 1:
            fn(x_refs[0], o_refs[0])
        else:
            pl.when(in_prompt)(lambda: fn(x_refs[0], o_refs[0]))
            pl.when(jnp.logical_not(in_prompt))(lambda: fn(x_refs[-1], o_refs[-1]))

    @pl.when(j == 0)
    def _():
        def load(x_ref, _):
            xb_ref[...] = x_ref[...].astype(BF16)
        per_group(load)
        acc_ref[...] = jnp.zeros_like(acc_ref)

    wg, wu, wo = wg_ref[...], wu_ref[...], wo_ref[...]
    if cast_weights:
        wg, wu, wo = wg.astype(BF16), wu.astype(BF16), wo.astype(BF16)
        for w_ref, w in zip(wb_refs, (wg, wu, wo)):
            w_ref[...] = w
    xb = xb_ref[...]
    gate = _dot(xb, wg)
    up = _dot(xb, wu)
    act = (gate * jax.nn.sigmoid(gate) * up).astype(BF16)
    acc_ref[...] += _dot(act, wo)

    @pl.when(j == pl.num_programs(1) - 1)
    def _():
        def finish(x_ref, o_ref):
            y = alpha * x_ref[...] + 0.5 * acc_ref[...]
            o_ref[...] = _layer_norm(y, g_ref[...], b_ref[...])
        per_group(finish)


def _ffn_ln(xs, out_rows, w_ffn_in, w_ffn_out, ln_g, ln_b, layer, which, ln_idx, alpha, tm, tf):
    D = xs[0].shape[1]
    T = sum(x.shape[0] for x in xs)
    F = w_ffn_out.shape[2]
    n_p = (xs[0].shape[0] if len(xs) == 2 else out_rows[0]) // tm
    ln_specs = [pl.BlockSpec((None, None, 1, D), lambda i, j: (layer, ln_idx, 0, 0))] * 2
    xb_scratch = pltpu.VMEM((tm, D), BF16)
    acc_scratch = [pltpu.VMEM((tm, D), F32)] if len(out_rows) == 2 else []
    kern = functools.partial(_ffn_ln_kernel, alpha=alpha, n_p=n_p)

    tf0 = tf // 2
    nf0 = F // tf0
    first = pl.pallas_call(
        functools.partial(kern, n_in=1, n_out=1, tile0=0, n_carried=0, cast_weights=True),
        grid=(1, nf0),
        in_specs=[
            pl.BlockSpec((tm, D), lambda i, j: (0, 0), pipeline_mode=pl.Buffered(1)),
            pl.BlockSpec((None, None, D, tf0), lambda i, j: (layer, which, 0, j)),
            pl.BlockSpec((None, None, D, tf0), lambda i, j: (layer, which, 0, j + nf0)),
            pl.BlockSpec((None, None, tf0, D), lambda i, j: (layer, which, j, 0)),
        ] + ln_specs,
        out_specs=[
            pl.BlockSpec((tm, D), lambda i, j: (0, 0)),
            pl.BlockSpec((D, tf0), lambda i, j: (0, j)),
            pl.BlockSpec((D, tf0), lambda i, j: (0, j)),
            pl.BlockSpec((tf0, D), lambda i, j: (j, 0)),
        ],
        out_shape=[
            jax.ShapeDtypeStruct((out_rows[0], D), F32),
            jax.ShapeDtypeStruct((D, F), BF16),
            jax.ShapeDtypeStruct((D, F), BF16),
            jax.ShapeDtypeStruct((F, D), BF16),
        ],
        scratch_shapes=[xb_scratch],
        compiler_params=_cparams(("arbitrary", "arbitrary")),
        name="ffn_ln_first",
    )
    y0, wg_b, wu_b, wo_b = first(xs[0], w_ffn_in, w_ffn_in, w_ffn_out, ln_g, ln_b)

    def row_specs(n):
        if n == 1:
            return [pl.BlockSpec((tm, D), lambda i, j: (i + 1, 0))]
        return [pl.BlockSpec((tm, D), lambda i, j: (jnp.minimum(i + 1, n_p - 1), 0)),
                pl.BlockSpec((tm, D), lambda i, j: (jnp.maximum(i + 1 - n_p, 0), 0))]

    nf = F // tf
    n_fixed = len(xs) + 5
    outs = pl.pallas_call(
        functools.partial(kern, n_in=len(xs), n_out=len(out_rows), tile0=1, n_carried=1,
                          cast_weights=False),
        grid=(T // tm - 1, nf),
        in_specs=row_specs(len(xs)) + [
            pl.BlockSpec((D, tf), lambda i, j: (0, j)),
            pl.BlockSpec((D, tf), lambda i, j: (0, j)),
            pl.BlockSpec((tf, D), lambda i, j: (j, 0)),
        ] + ln_specs + [pl.BlockSpec(memory_space=pl.ANY)],
        out_specs=row_specs(len(out_rows)),
        out_shape=[jax.ShapeDtypeStruct((r, D), F32) for r in out_rows],
        input_output_aliases={n_fixed: 0},
        scratch_shapes=[xb_scratch] + acc_scratch,
        compiler_params=_cparams(("arbitrary", "arbitrary")),
        name="ffn_ln",
    )(*xs, wg_b, wu_b, wo_b, ln_g, ln_b, y0)
    return list(outs)


def _proj_row_prologue(x_ref, xb_ref, wa, aup_ref, ab_ref, la_ref):
    xb = x_ref[...].astype(BF16)
    xb_ref[...] = xb
    a_lo = _dot(xb, wa)
    z = _dot(a_lo.astype(BF16), aup_ref[...]) + ab_ref[...]
    la_ref[...] = (jnp.minimum(z, 0.0) - jnp.log(1.0 + jnp.exp(-jnp.abs(z)))) / GLA_TAU


def _proj_first_kernel(x_ref, wsrc_ref, wnext_ref, walo_ref, aup_ref, ab_ref, h_ref, la_ref, wm_ref,
                       wa_ref, xb_ref, *, n_shift, rank):
    j = pl.program_id(1)
    tn = wsrc_ref.shape[1]

    @pl.when(j == 0)
    def _():
        lane = lax.broadcasted_iota(jnp.int32, walo_ref.shape, 1)
        wa = jnp.where(lane < rank, walo_ref[...], 0.0).astype(BF16)
        wa_ref[...] = wa
        _proj_row_prologue(x_ref, xb_ref, wa, aup_ref, ab_ref, la_ref)

    def emit(w):
        wm_ref[...] = w
        h_ref[...] = _dot(xb_ref[...], w)

    @pl.when(j < n_shift)
    def _():
        both = jnp.concatenate([wsrc_ref[...], wnext_ref[...]], axis=1)
        emit(pltpu.roll(both, both.shape[1] - rank, axis=1)[:, :tn].astype(BF16))

    @pl.when(j >= n_shift)
    def _():
        emit(wsrc_ref[...].astype(BF16))


def _proj_kernel(x_ref, wm_ref, wa_ref, aup_ref, ab_ref, h_in_ref, la_in_ref, h_ref, la_ref, xb_ref):
    del h_in_ref, la_in_ref
    j = pl.program_id(1)

    @pl.when(j == 0)
    def _():
        _proj_row_prologue(x_ref, xb_ref, wa_ref[...], aup_ref, ab_ref, la_ref)

    h_ref[...] = _dot(xb_ref[...], wm_ref[...])


def _proj(x, w_in, aup, abias, layer, tm, tn, seg):
    T, D = x.shape
    GK = aup.shape[2]
    head, mid, rank, tail = seg
    lo0 = head + mid
    NH = lo0 + tail
    tn0 = tn // 2
    assert lo0 % tn0 == 0 and head % tn0 == 0 and tail % tn0 == 0 and lo0 % V7X_LANES == 0
    n_shift, n_al = tail // tn0, lo0 // tn0
    per_tile = tn0 // V7X_LANES

    def src_block(j):
        return jnp.where(j < n_shift, n_al + j, (j - n_shift + head // tn0) % n_al)

    def next_block(j):
        return (n_al + jnp.minimum(j, n_shift - 1) + 1) * per_tile

    const = lambda i, j: (layer, 0, 0)
    h0, la0, wm, wa = pl.pallas_call(
        functools.partial(_proj_first_kernel, n_shift=n_shift, rank=rank),
        grid=(1, NH // tn0),
        in_specs=[
            pl.BlockSpec((tm, D), lambda i, j: (0, 0), pipeline_mode=pl.Buffered(1)),
            pl.BlockSpec((None, D, tn0), lambda i, j: (layer, 0, src_block(j))),
            pl.BlockSpec((None, D, V7X_LANES), lambda i, j: (layer, 0, next_block(j))),
            pl.BlockSpec((None, D, V7X_LANES), lambda i, j: (layer, 0, lo0 // V7X_LANES)),
            pl.BlockSpec((None, V7X_LANES, GK), const),
            pl.BlockSpec((None, 1, GK), const),
        ],
        out_specs=[
            pl.BlockSpec((tm, tn0), lambda i, j: (0, j)),
            pl.BlockSpec((tm, GK), lambda i, j: (0, 0)),
            pl.BlockSpec((D, tn0), lambda i, j: (0, j)),
            pl.BlockSpec((D, V7X_LANES), lambda i, j: (0, 0)),
        ],
        out_shape=[
            jax.ShapeDtypeStruct((T, NH), F32), jax.ShapeDtypeStruct((T, GK), F32),
            jax.ShapeDtypeStruct((D, NH), BF16), jax.ShapeDtypeStruct((D, V7X_LANES), BF16),
        ],
        scratch_shapes=[pltpu.VMEM((tm, D), BF16)],
        compiler_params=_cparams(("arbitrary", "arbitrary")),
        name="proj_first",
    )(x, w_in, w_in, w_in, aup, abias)

    return pl.pallas_call(
        _proj_kernel,
        grid=(T // tm - 1, NH // tn),
        in_specs=[
            pl.BlockSpec((tm, D), lambda i, j: (i + 1, 0)),
            pl.BlockSpec((D, tn), lambda i, j: (0, j)),
            pl.BlockSpec((D, V7X_LANES), lambda i, j: (0, 0)),
            pl.BlockSpec((None, V7X_LANES, GK), const),
            pl.BlockSpec((None, 1, GK), const),
            pl.BlockSpec(memory_space=pl.ANY),
            pl.BlockSpec(memory_space=pl.ANY),
        ],
        out_specs=[
            pl.BlockSpec((tm, tn), lambda i, j: (i + 1, j)),
            pl.BlockSpec((tm, GK), lambda i, j: (i + 1, 0)),
        ],
        out_shape=[jax.ShapeDtypeStruct((T, NH), F32), jax.ShapeDtypeStruct((T, GK), F32)],
        input_output_aliases={5: 0, 6: 1},
        scratch_shapes=[pltpu.VMEM((tm, D), BF16)],
        compiler_params=_cparams(("parallel", "arbitrary")),
        name="proj",
    )(x, wm, wa, aup, abias, h0, la0)


def _stack01(mats, dtype):
    return jnp.asarray(np.concatenate(mats, axis=0).astype(np.float32), dtype=dtype)


def _cumsum_mat(rows):
    i = np.arange(rows)[:, None]
    j = np.arange(rows)[None, :]
    return _stack01([j <= i], BF16)


def _seq_cumsum_mats(rows, seq):
    i = np.arange(rows)[:, None]
    j = np.arange(rows)[None, :]
    same = i // seq == j // seq
    return _stack01([same & (j <= i), same], BF16)


def _level_masks(rows, groups):
    i = np.arange(rows)[:, None]
    j = np.arange(rows)[None, :]
    out = []
    for grp in groups:
        half = grp // 2
        out.append((i // grp == j // grp) & (i % grp >= half) & (j % grp < half))
    return _stack01(out, F32)


def _intra_chunk_scores(qs, k, b, groups, lmask_ref, sub):
    C = qs.shape[0]
    A = jnp.zeros((C, C), F32)
    for n, grp in enumerate(groups):
        e = jnp.concatenate(
            [jnp.exp2(-jnp.abs(b[s:s + grp] - b[s + grp // 2 - 1:s + grp // 2])) for s in range(0, C, grp)],
            axis=0)
        A = A + _dot_nt((qs * e).astype(BF16), (k * e).astype(BF16)) * lmask_ref[n * C:(n + 1) * C, :]
    row_in_sub = lax.broadcasted_iota(jnp.int32, (C, 1), 0) & (sub - 1)
    delta = lax.broadcasted_iota(jnp.int32, (C, C), 0) - lax.broadcasted_iota(jnp.int32, (C, C), 1)
    for d in range(sub):
        if d == 0:
            t = qs * k
        else:
            kd = pltpu.roll(k, d, axis=0)
            bd = pltpu.roll(b, d, axis=0)
            t = qs * kd * jnp.exp2(jnp.minimum(b - bd, 0.0))
        red = jnp.sum(t, axis=1, keepdims=True)
        red = jnp.where(row_in_sub >= d, red, 0.0)
        A = A + jnp.where(delta == d, red, 0.0)
    return A


def _head_norm(o):
    return o * lax.rsqrt(jnp.mean(o * o, axis=-1, keepdims=True) + HEAD_NORM_EPS)


def _gla_prompt_kernel(q_ref, k_ref, v_ref, la_ref, cmat_ref, lmask_ref, o_ref, sout_ref, s_ref,
                       *, scale, sub, groups):
    t = pl.program_id(2)
    C = q_ref.shape[0]
    HB, DK, DV = s_ref.shape

    @pl.when(t == 0)
    def _():
        s_ref[...] = jnp.zeros_like(s_ref)

    cm = cmat_ref[...]
    p0, p1, p2 = _split3(la_ref[...] * LOG2_E)
    b_all = _dot(cm, p0) + _dot(cm, p1) + _dot(cm, p2)
    for hh in range(HB):
        kc = slice(hh * DK, (hh + 1) * DK)
        vc = slice(hh * DV, (hh + 1) * DV)
        qs = q_ref[:, kc] * scale
        k = k_ref[:, kc]
        vb = v_ref[:, vc].astype(BF16)
        b = b_all[:, kc]

        A = _intra_chunk_scores(qs, k, b, groups, lmask_ref, sub)
        S = s_ref[hh]
        o = _dot((qs * jnp.exp2(b)).astype(BF16), S.astype(BF16)) + _dot(A.astype(BF16), vb)
        o_ref[:, vc] = _head_norm(o)

        bl = b[C - 1:C, :]
        ku_t = (k * jnp.exp2(bl - b)).T
        d_s = _dot(ku_t.astype(BF16), vb)
        a_t = jnp.exp2(jnp.broadcast_to(bl, (V7X_LANES, DK))).T
        for n in range(DV // V7X_LANES):
            sl = slice(n * V7X_LANES, (n + 1) * V7X_LANES)
            s_ref[hh, :, sl] = a_t * S[:, sl] + d_s[:, sl]

    @pl.when(t == pl.num_programs(2) - 1)
    def _():
        sout_ref[...] = s_ref[...]


def _gla_prompt(h, la, batch, seq, heads, dk, dv, q_off, k_off, v_off):
    C = GLA_CHUNK_ROWS
    HB = GLA_HEADS_PER_STEP
    nt = seq // C
    groups = []
    g = C
    while g > GLA_SUB:
        groups.append(g)
        g //= 2
    cmat = _cumsum_mat(C)
    lmask = _level_masks(C, groups)
    wk, wv = HB * dk, HB * dv
    assert heads % HB == 0 and q_off % wk == 0 and k_off % wk == 0 and v_off % wv == 0
    qb, kb, vb = q_off // wk, k_off // wk, v_off // wv
    return pl.pallas_call(
        functools.partial(_gla_prompt_kernel, scale=float(dk) ** -0.5, sub=GLA_SUB,
                          groups=tuple(groups)),
        grid=(batch, heads // HB, nt),
        in_specs=[
            pl.BlockSpec((C, wk), lambda bi, hi, t: (bi * nt + t, qb + hi)),
            pl.BlockSpec((C, wk), lambda bi, hi, t: (bi * nt + t, kb + hi)),
            pl.BlockSpec((C, wv), lambda bi, hi, t: (bi * nt + t, vb + hi)),
            pl.BlockSpec((C, wk), lambda bi, hi, t: (bi * nt + t, hi)),
            pl.BlockSpec(cmat.shape, lambda bi, hi, t: (0, 0)),
            pl.BlockSpec(lmask.shape, lambda bi, hi, t: (0, 0)),
        ],
        out_specs=[
            pl.BlockSpec((C, wv), lambda bi, hi, t: (bi * nt + t, hi)),
            pl.BlockSpec((None, HB, dk, dv), lambda bi, hi, t: (bi, hi, 0, 0)),
        ],
        out_shape=[
            jax.ShapeDtypeStruct((batch * seq, heads * dv), F32),
            jax.ShapeDtypeStruct((batch, heads, dk, dv), F32),
        ],
        scratch_shapes=[pltpu.VMEM((HB, dk, dv), F32)],
        compiler_params=_cparams(("parallel", "parallel", "arbitrary")),
        name="gla_prompt",
    )(h, h, h, la, cmat, lmask)


def _gla_sample_kernel(q_ref, k_ref, v_ref, la_ref, cmat_ref, sin_ref, *rest, scale, seq):
    o_ref, sout_ref = rest[-2:]
    R, DK = q_ref.shape
    DV = v_ref.shape[1]
    nb = R // seq

    qs = q_ref[...] * scale
    k = k_ref[...]
    vb = v_ref[...].astype(BF16)
    cm = cmat_ref[...]
    p0, p1, p2 = _split3(la_ref[...] * LOG2_E)
    ball = _dot(cm, p0) + _dot(cm, p1) + _dot(cm, p2)
    b = ball[0:R]
    bl = ball[R:2 * R]

    A = _intra_chunk_scores(qs, k, b, (), None, seq)
    o_intra = _dot(A.astype(BF16), vb)
    qo = qs * jnp.exp2(b)
    ku_t = (k * jnp.exp2(bl - b)).T
    lane_seq = lax.broadcasted_iota(jnp.int32, (DK, R), 1) // seq
    for i in range(nb):
        rows = slice(i * seq, (i + 1) * seq)
        S = sin_ref[i]
        o = _dot(qo[rows].astype(BF16), S.astype(BF16)) + o_intra[rows]
        o_ref[rows, :] = _head_norm(o)
        d_s = _dot(jnp.where(lane_seq == i, ku_t, 0.0).astype(BF16), vb)
        a_t = jnp.exp2(jnp.broadcast_to(bl[i * seq:i * seq + 1, :], (V7X_LANES, DK))).T
        for n in range(DV // V7X_LANES):
            sl = slice(n * V7X_LANES, (n + 1) * V7X_LANES)
            sout_ref[i, :, sl] = a_t * S[:, sl] + d_s[:, sl]


def _gla_sample(h, la, state, new_state, layer, row0, batch, seq, heads, dk, dv, q_off, k_off, v_off):
    R = V7X_LANES
    nb = R // seq
    ng = batch // nb
    rb = row0 // R
    cmat = _seq_cumsum_mats(R, seq)
    qb, kb, vb = q_off // dk, k_off // dk, v_off // dv
    carried = [] if new_state is None else [new_state]
    return pl.pallas_call(
        functools.partial(_gla_sample_kernel, scale=float(dk) ** -0.5, seq=seq),
        grid=(ng, heads),
        in_specs=[
            pl.BlockSpec((R, dk), lambda g, hi: (rb + g, qb + hi)),
            pl.BlockSpec((R, dk), lambda g, hi: (rb + g, kb + hi)),
            pl.BlockSpec((R, dv), lambda g, hi: (rb + g, vb + hi)),
            pl.BlockSpec((R, dk), lambda g, hi: (rb + g, hi)),
            pl.BlockSpec(cmat.shape, lambda g, hi: (0, 0)),
            pl.BlockSpec((None, nb, None, dk, dv), lambda g, hi: (layer, g, hi, 0, 0)),
        ] + [pl.BlockSpec(memory_space=pl.ANY)] * len(carried),
        out_specs=[
            pl.BlockSpec((R, dv), lambda g, hi: (g, hi)),
            pl.BlockSpec((None, nb, None, dk, dv), lambda g, hi: (layer, g, hi, 0, 0)),
        ],
        out_shape=[
            jax.ShapeDtypeStruct((batch * seq, heads * dv), F32),
            jax.ShapeDtypeStruct(state.shape, F32),
        ],
        input_output_aliases={6: 1} if carried else {},
        compiler_params=_cparams(("parallel", "parallel")),
        name="gla_sample",
    )(h, h, h, la, cmat, state, *carried)


def _window_means_minus_input(ext, u, pos, lead):
    n, pw = u.shape
    gw = pw // len(POOL_WINDOWS)
    outs = []
    for g, w in enumerate(POOL_WINDOWS):
        cols = slice(g * gw, (g + 1) * gw)
        p = ext[:, cols]
        span = 1
        while span < w:
            p = p + pltpu.roll(p, span, axis=0)
            span *= 2
        cnt = jnp.minimum(w, pos + 1).astype(F32)
        outs.append(p[lead:lead + n] / cnt - u[:, cols])
    return outs


def _mix_tail(pooled, on, r, ga, gb, x, hg_ref, wgo_ref, pw_ref, ps_ref, wo_ref, g_ref, b_ref, alpha):
    ya = jnp.concatenate(
        [_dot(p.astype(BF16), pw_ref[g]) for g, p in enumerate(pooled)], axis=1) * ps_ref[...]
    yb = _dot((on * hg_ref[...] * (r * jax.nn.sigmoid(r))).astype(BF16), wgo_ref[...])
    merged = jax.nn.sigmoid(ga) * ya + jax.nn.sigmoid(gb) * yb
    m = _dot(merged.astype(BF16), wo_ref[...])
    return _layer_norm(alpha * x + m, g_ref[...], b_ref[...])


def _post_prompt_kernel(on_ref, r_ref, ga_ref, gb_ref, u_ref, x_ref, hg_ref, wgo_ref, pw_ref, ps_ref,
                        wo_ref, g_ref, b_ref, o_ref, npool_ref, halo_ref, *, tiles_per_seq, alpha):
    i = pl.program_id(0)
    tm = u_ref.shape[0]
    lead = halo_ref.shape[0]
    ti = i % tiles_per_seq

    @pl.when(ti == 0)
    def _():
        halo_ref[...] = jnp.zeros_like(halo_ref)

    u = u_ref[...]
    ext = jnp.concatenate([halo_ref[...], u], axis=0)
    halo_ref[...] = u[tm - lead:, :]

    @pl.when(ti == tiles_per_seq - 1)
    def _():
        npool_ref[...] = u[tm - POOL_BUF:, :]

    pos = ti * tm + lax.broadcasted_iota(jnp.int32, (tm, 1), 0)
    pooled = _window_means_minus_input(ext, u, pos, lead)
    o_ref[...] = _mix_tail(pooled, on_ref[...], r_ref[...], ga_ref[...], gb_ref[...], x_ref[...],
                           hg_ref, wgo_ref, pw_ref, ps_ref, wo_ref, g_ref, b_ref, alpha)


def _post_sample_kernel(on_ref, r_ref, ga_ref, gb_ref, u_ref, x_ref, st_ref, hg_ref, wgo_ref, pw_ref,
                        ps_ref, wo_ref, g_ref, b_ref, carried_ref, o_ref, npool_ref, ext_ref,
                        *, seq, start, alpha):
    del carried_ref
    tm, pw = u_ref.shape
    nb = tm // seq
    lead = ext_ref.shape[1] - seq
    u = u_ref[...]
    u3 = u.reshape(nb, seq, pw)
    st = st_ref[...]
    ext_ref[:, 0:lead - POOL_BUF, :] = jnp.zeros((nb, lead - POOL_BUF, pw), F32)
    ext_ref[:, lead - POOL_BUF:lead, :] = st
    ext_ref[:, lead:, :] = u3
    e3 = ext_ref[...]
    npool_ref[...] = e3[:, lead + seq - POOL_BUF:, :]
    p = e3.reshape(nb * (lead + seq), pw)
    gw = pw // len(POOL_WINDOWS)
    pos = start + lax.broadcasted_iota(jnp.int32, (nb, seq, 1), 1)
    pooled = []
    for g, w in enumerate(POOL_WINDOWS):
        cols = slice(g * gw, (g + 1) * gw)
        pg = p[:, cols]
        span = 1
        while span < w:
            pg = pg + pltpu.roll(pg, span, axis=0)
            span *= 2
        win = pg.reshape(nb, lead + seq, gw)[:, lead:, :]
        cnt = jnp.minimum(w, pos + 1).astype(F32)
        pooled.append((win / cnt - u3[:, :, cols]).reshape(tm, gw))
    o_ref[...] = _mix_tail(pooled, on_ref[...], r_ref[...], ga_ref[...], gb_ref[...], x_ref[...],
                           hg_ref, wgo_ref, pw_ref, ps_ref, wo_ref, g_ref, b_ref, alpha)


def _const_spec(shape, index_map):
    return pl.BlockSpec(shape, index_map, pipeline_mode=pl.Buffered(1))


def _post_weight_specs(layer, ln_idx, D, GV, groups, gw, ogw, n):
    z = (0,) * (n - 1)
    return [
        _const_spec((None, 1, GV), lambda *a: (layer, 0, 0)),
        _const_spec((None, GV, D), lambda *a: (layer, 0, 0)),
        _const_spec((None, groups, gw, ogw), lambda *a: (layer, 0, 0, 0)),
        _const_spec((None, 1, D), lambda *a: (layer, 0, 0)),
        _const_spec((None, D, D), lambda *a: (layer, 0, 0)),
        _const_spec((None, None, 1, D), lambda *a: (layer, ln_idx, 0, 0)),
        _const_spec((None, None, 1, D), lambda *a: (layer, ln_idx, 0, 0)),
    ]


def _post_prompt(on, h, x, weights, layer, ln_idx, alpha, seq, tm, cols):
    TP, GV = on.shape
    T, D = x.shape
    hg, wgo, pw, ps, wo, ln_g, ln_b = weights
    groups, gw, ogw = pw.shape[1:]
    PW = groups * gw
    tiles_per_seq = seq // tm
    r_b, ga_b, gb_b, u_b = cols["r"] // GV, cols["ga"] // D, cols["gb"] // D, cols["u"] // PW
    lead = 2 * V7X_SUBLANES
    return pl.pallas_call(
        functools.partial(_post_prompt_kernel, tiles_per_seq=tiles_per_seq, alpha=alpha),
        grid=(TP // tm,),
        in_specs=[
            pl.BlockSpec((tm, GV), lambda i: (i, 0)),
            pl.BlockSpec((tm, GV), lambda i: (i, r_b)),
            pl.BlockSpec((tm, D), lambda i: (i, ga_b)),
            pl.BlockSpec((tm, D), lambda i: (i, gb_b)),
            pl.BlockSpec((tm, PW), lambda i: (i, u_b)),
            pl.BlockSpec((tm, D), lambda i: (i, 0)),
        ] + _post_weight_specs(layer, ln_idx, D, GV, groups, gw, ogw, 1),
        out_specs=[
            pl.BlockSpec((tm, D), lambda i: (i, 0)),
            pl.BlockSpec((None, POOL_BUF, PW), lambda i: (i // tiles_per_seq, 0, 0)),
        ],
        out_shape=[
            jax.ShapeDtypeStruct((T, D), F32),
            jax.ShapeDtypeStruct((TP // seq, POOL_BUF, PW), F32),
        ],
        scratch_shapes=[pltpu.VMEM((lead, PW), F32)],
        compiler_params=_cparams(("arbitrary",)),
        name="post_prompt",
    )(on, h, h, h, h, x, hg, wgo, pw, ps, wo, ln_g, ln_b)


def _post_sample(on, h, x, x2, state_pool, weights, layer, ln_idx, alpha, row0, seq, tm, cols):
    TS, GV = on.shape
    D = x.shape[1]
    hg, wgo, pw, ps, wo, ln_g, ln_b = weights
    groups, gw, ogw = pw.shape[1:]
    PW = groups * gw
    nb = tm // seq
    batch = TS // seq
    rb = row0 // tm
    r_b, ga_b, gb_b, u_b = cols["r"] // GV, cols["ga"] // D, cols["gb"] // D, cols["u"] // PW
    lead = 2 * V7X_SUBLANES
    return pl.pallas_call(
        functools.partial(_post_sample_kernel, seq=seq, start=PAST_LEN, alpha=alpha),
        grid=(TS // tm,),
        in_specs=[
            pl.BlockSpec((tm, GV), lambda i: (i, 0)),
            pl.BlockSpec((tm, GV), lambda i: (rb + i, r_b)),
            pl.BlockSpec((tm, D), lambda i: (rb + i, ga_b)),
            pl.BlockSpec((tm, D), lambda i: (rb + i, gb_b)),
            pl.BlockSpec((tm, PW), lambda i: (rb + i, u_b)),
            pl.BlockSpec((tm, D), lambda i: (rb + i, 0)),
            pl.BlockSpec((None, nb, POOL_BUF, PW), lambda i: (layer, i, 0, 0)),
        ] + _post_weight_specs(layer, ln_idx, D, GV, groups, gw, ogw, 1) + [
            pl.BlockSpec(memory_space=pl.ANY),
        ],
        out_specs=[
            pl.BlockSpec((tm, D), lambda i: (rb + i, 0)),
            pl.BlockSpec((nb, POOL_BUF, PW), lambda i: (i, 0, 0)),
        ],
        out_shape=[
            jax.ShapeDtypeStruct(x2.shape, F32),
            jax.ShapeDtypeStruct((batch, POOL_BUF, PW), F32),
        ],
        input_output_aliases={14: 0},
        scratch_shapes=[pltpu.VMEM((nb, lead + seq, PW), F32)],
        compiler_params=_cparams(("arbitrary",)),
        name="post_sample",
    )(on, h, h, h, h, x, state_pool, hg, wgo, pw, ps, wo, ln_g, ln_b, x2)


def _tiles(T, TP, seq_p, F, NH):
    def fit(n, t):
        while n % t:
            t //= 2
        return t
    return dict(
        ffn_tm=fit(np.gcd(T, TP), 512), ffn_tm_big=fit(np.gcd(T, TP), 1024), ffn_tf=fit(F, 512),
        proj_tm=fit(np.gcd(T, TP), 1024), proj_tn=fit(NH, 1024),
        post_tm=fit(seq_p, 256), post_sample_tm=fit(np.gcd(T - TP, TP), 128),
    )


def kernel(x_prompt, x_sample, state_pool, state_gla, ln_g, ln_b, w_ffn_in, w_ffn_out, w_in, pool_w,
           pool_scale, a_up, a_bias, head_g, w_gla_out, w_out):
    B, SEQ, D = x_prompt.shape
    DB, DSEQ, _ = x_sample.shape
    L = w_in.shape[0]
    PW = state_pool.shape[-1]
    H, DK, DV = state_gla.shape[2:]
    GK, GV = H * DK, H * DV
    RANK = a_up.shape[1]
    F = w_ffn_out.shape[2]
    TP, TS = B * SEQ, DB * DSEQ
    T = TP + TS
    alpha = float((2 * L) ** 0.25)
    assert SEQ % GLA_CHUNK_ROWS == 0 and TP % V7X_LANES == 0 and V7X_LANES % DSEQ == 0
    assert DSEQ == V7X_SUBLANES and PAST_LEN >= POOL_BUF

    seg = (PW + 2 * GK, GV, RANK, GV + 2 * D)
    assert sum(seg) == w_in.shape[2]
    aup = jnp.pad(a_up, ((0, 0), (0, V7X_LANES - RANK), (0, 0))).astype(BF16)
    abias = a_bias.reshape(L, 1, GK)
    ln_g4 = ln_g.reshape(L, 3, 1, D)
    ln_b4 = ln_b.reshape(L, 3, 1, D)
    post_w = (head_g.reshape(L, 1, GV), w_gla_out.astype(BF16), pool_w.astype(BF16),
              pool_scale.reshape(L, 1, D), w_out.astype(BF16), ln_g4, ln_b4)
    c_v = GV + 2 * D
    c_u = c_v + GV
    cols = dict(r=0, ga=GV, gb=GV + D, v=c_v, u=c_u, q=c_u + PW, k=c_u + PW + GK)
    ts = _tiles(T, TP, SEQ, F, w_in.shape[2] - RANK)
    assert GV == D and c_v % DV == 0 and c_u % PW == 0 and cols["q"] % DK == 0

    def ffn(xs, out_rows, **kw):
        tm = ts["ffn_tm_big"] if len(xs) == 1 and len(out_rows) == 1 else ts["ffn_tm"]
        return _ffn_ln(xs, out_rows, w_ffn_in, w_ffn_out, ln_g4, ln_b4, alpha=alpha, tm=tm,
                       tf=ts["ffn_tf"], **kw)

    xs = [x_prompt.reshape(TP, D), x_sample.reshape(TS, D)]
    pool_p, gla_p, pool_s, gla_s = [], [], [], None
    for l in range(L):
        (x,) = ffn(xs, [T], layer=l, which=0, ln_idx=0)
        h, la = _proj(x, w_in, aup, abias, l, ts["proj_tm"], ts["proj_tn"], seg)
        on_p, s_p = _gla_prompt(h, la, B, SEQ, H, DK, DV, cols["q"], cols["k"], cols["v"])
        on_s, gla_s = _gla_sample(h, la, state_gla, gla_s, l, TP, DB, DSEQ, H, DK, DV,
                                  cols["q"], cols["k"], cols["v"])
        x2, np_p = _post_prompt(on_p, h, x, post_w, l, 1, alpha, SEQ, ts["post_tm"], cols)
        x2, np_s = _post_sample(on_s, h, x, x2, state_pool, post_w, l, 1, alpha, TP, DSEQ,
                                ts["post_sample_tm"], cols)
        xs = ffn([x2], [T] if l + 1 < L else [TP, TS], layer=l, which=1, ln_idx=2)
        pool_p.append(np_p)
        gla_p.append(s_p)
        pool_s.append(np_s)
    return (xs[0].reshape(B, SEQ, D), xs[1].reshape(DB, DSEQ, D), jnp.stack(pool_p),
            jnp.stack(gla_p), jnp.stack(pool_s), gla_s)
```

```python
import functools

import numpy as np
import jax
import jax.numpy as jnp
from jax import lax
from jax.experimental import pallas as pl
from jax.experimental.pallas import tpu as pltpu

POOL_WINDOWS = (2, 4, 8, 16)
POOL_BUF = max(POOL_WINDOWS) - 1
GLA_TAU = 16.0
LN_EPS = 1e-5
HEAD_NORM_EPS = 1e-6
PAST_LEN = 16384

V7X_LANES = 128
V7X_SUBLANES = 8
V7X_VMEM_BYTES = 64 * 1024 * 1024
V7X_VMEM_LIMIT_BYTES = V7X_VMEM_BYTES - 1024 * 1024

F32 = jnp.float32
BF16 = jnp.bfloat16

GLA_SUB = 4
GLA_CHUNK_ROWS = 128
GLA_HEADS_PER_STEP = 4
LOG2_E = 1.4426950408889634


def _cparams(sem):
    return pltpu.CompilerParams(dimension_semantics=sem, vmem_limit_bytes=V7X_VMEM_LIMIT_BYTES)


def _layer_norm(y, g, b):
    mu = jnp.mean(y, axis=-1, keepdims=True)
    yc = y - mu
    var = jnp.mean(yc * yc, axis=-1, keepdims=True)
    return yc * lax.rsqrt(var + LN_EPS) * g + b


def _dot(a, b):
    return jnp.dot(a, b, preferred_element_type=F32)


def _dot_nt(a, b):
    return lax.dot_general(a, b, (((1,), (1,)), ((), ())), preferred_element_type=F32)


def _split3(x):
    p0 = x.astype(BF16)
    r1 = x - p0.astype(F32)
    p1 = r1.astype(BF16)
    p2 = (r1 - p1.astype(F32)).astype(BF16)
    return p0, p1, p2


FFN_DOWN_CHUNKS = 4


def _ffn_ln_kernel(*refs, alpha, n_carried, cast_weights):
    refs = list(refs)
    x_ref, wg_ref, wu_ref, wo_ref, g_ref, b_ref = [refs.pop(0) for _ in range(6)]
    del refs[:n_carried]
    o_ref = refs.pop(0)
    wb_refs = [refs.pop(0) for _ in range(3)] if cast_weights else None
    (xb_ref,) = refs
    j = pl.program_id(1)

    @pl.when(j == 0)
    def _():
        xb_ref[...] = x_ref[...].astype(BF16)
        o_ref[...] = jnp.zeros_like(o_ref)

    wg, wu, wo = wg_ref[...], wu_ref[...], wo_ref[...]
    if cast_weights:
        wg, wu, wo = wg.astype(BF16), wu.astype(BF16), wo.astype(BF16)
        for w_ref, w in zip(wb_refs, (wg, wu, wo)):
            w_ref[...] = w
    xb = xb_ref[...]
    gate = _dot(xb, wg)
    up = _dot(xb, wu)
    act = (gate * jax.nn.sigmoid(gate) * up).astype(BF16)
    cw = o_ref.shape[1] // FFN_DOWN_CHUNKS
    for c in range(FFN_DOWN_CHUNKS):
        cs = slice(c * cw, (c + 1) * cw)
        o_ref[:, cs] += _dot(act, wo[:, cs])

    @pl.when(j == pl.num_programs(1) - 1)
    def _():
        y = alpha * x_ref[...] + 0.5 * o_ref[...]
        o_ref[...] = _layer_norm(y, g_ref[...], b_ref[...])


def _ffn_ln(xs, out_rows, w_ffn_in, w_ffn_out, ln_g, ln_b, layer, which, ln_idx, alpha, tm, tf):
    D = xs[0].shape[1]
    F = w_ffn_out.shape[2]
    ln_specs = [pl.BlockSpec((None, None, 1, D), lambda i, j: (layer, ln_idx, 0, 0))] * 2
    xb_scratch = pltpu.VMEM((tm, D), BF16)

    def tile_owner(rows_list):
        return [(a, t) for a, rows in enumerate(rows_list) for t in range(rows // tm)]
    runs = []
    for (sa, st), (da, dt) in zip(tile_owner([x.shape[0] for x in xs]), tile_owner(out_rows)):
        if runs and runs[-1][0] == sa and runs[-1][2] == da:
            runs[-1][4] += 1
        else:
            runs.append([sa, st, da, dt, 1])

    tf0 = tf // 2
    nf0 = F // tf0
    sa, st, da, dt, _ = runs[0]
    y0, wg_b, wu_b, wo_b = pl.pallas_call(
        functools.partial(_ffn_ln_kernel, alpha=alpha, n_carried=0, cast_weights=True),
        grid=(1, nf0),
        in_specs=[
            pl.BlockSpec((tm, D), lambda i, j: (st, 0), pipeline_mode=pl.Buffered(1)),
            pl.BlockSpec((None, None, D, tf0), lambda i, j: (layer, which, 0, j)),
            pl.BlockSpec((None, None, D, tf0), lambda i, j: (layer, which, 0, j + nf0)),
            pl.BlockSpec((None, None, tf0, D), lambda i, j: (layer, which, j, 0)),
        ] + ln_specs,
        out_specs=[
            pl.BlockSpec((tm, D), lambda i, j: (dt, 0)),
            pl.BlockSpec((D, tf0), lambda i, j: (0, j)),
            pl.BlockSpec((D, tf0), lambda i, j: (0, j)),
            pl.BlockSpec((tf0, D), lambda i, j: (j, 0)),
        ],
        out_shape=[
            jax.ShapeDtypeStruct((out_rows[da], D), F32),
            jax.ShapeDtypeStruct((D, F), BF16),
            jax.ShapeDtypeStruct((D, F), BF16),
            jax.ShapeDtypeStruct((F, D), BF16),
        ],
        scratch_shapes=[xb_scratch],
        compiler_params=_cparams(("arbitrary", "arbitrary")),
        name="ffn_ln_first",
    )(xs[sa], w_ffn_in, w_ffn_in, w_ffn_out, ln_g, ln_b)
    outs = {da: y0}
    runs[0][1] += 1
    runs[0][3] += 1
    runs[0][4] -= 1

    nf = F // tf
    for sa, st, da, dt, n in runs:
        if n == 0:
            continue
        carried = [outs[da]] if da in outs else []
        outs[da] = pl.pallas_call(
            functools.partial(_ffn_ln_kernel, alpha=alpha, n_carried=len(carried), cast_weights=False),
            grid=(n, nf),
            in_specs=[
                pl.BlockSpec((tm, D), lambda i, j, st=st: (i + st, 0)),
                pl.BlockSpec((D, tf), lambda i, j: (0, j)),
                pl.BlockSpec((D, tf), lambda i, j: (0, j)),
                pl.BlockSpec((tf, D), lambda i, j: (j, 0)),
            ] + ln_specs + [pl.BlockSpec(memory_space=pl.ANY)] * len(carried),
            out_specs=pl.BlockSpec((tm, D), lambda i, j, dt=dt: (i + dt, 0)),
            out_shape=jax.ShapeDtypeStruct((out_rows[da], D), F32),
            input_output_aliases={6: 0} if carried else {},
            scratch_shapes=[xb_scratch],
            compiler_params=_cparams(("arbitrary", "arbitrary")),
            name="ffn_ln",
        )(xs[sa], wg_b, wu_b, wo_b, ln_g, ln_b, *carried)
    return [outs[a] for a in range(len(out_rows))]


def _proj_row_prologue(x_ref, xb_ref, wa, aup_ref, ab_ref, la_ref):
    xb = x_ref[...].astype(BF16)
    xb_ref[...] = xb
    a_lo = _dot(xb, wa)
    z = _dot(a_lo.astype(BF16), aup_ref[...]) + ab_ref[...]
    la_ref[...] = (jnp.minimum(z, 0.0) - jnp.log(1.0 + jnp.exp(-jnp.abs(z)))) / GLA_TAU


def _proj_first_kernel(x_ref, wsrc_ref, wnext_ref, walo_ref, aup_ref, ab_ref, h_ref, la_ref, wm_ref,
                       wa_ref, xb_ref, *, n_shift, rank):
    j = pl.program_id(1)
    tn = wsrc_ref.shape[1]

    @pl.when(j == 0)
    def _():
        lane = lax.broadcasted_iota(jnp.int32, walo_ref.shape, 1)
        wa = jnp.where(lane < rank, walo_ref[...].astype(F32), 0.0).astype(BF16)
        wa_ref[...] = wa
        _proj_row_prologue(x_ref, xb_ref, wa, aup_ref, ab_ref, la_ref)

    def emit(w):
        wm_ref[...] = w
        h_ref[...] = _dot(xb_ref[...], w)

    @pl.when(j < n_shift)
    def _():
        both = jnp.concatenate([wsrc_ref[...], wnext_ref[...]], axis=1).astype(F32)
        emit(pltpu.roll(both, both.shape[1] - rank, axis=1)[:, :tn].astype(BF16))

    @pl.when(j >= n_shift)
    def _():
        emit(wsrc_ref[...])


def _proj_kernel(x_ref, wm_ref, wa_ref, aup_ref, ab_ref, h_in_ref, la_in_ref, h_ref, la_ref, xb_ref):
    del h_in_ref, la_in_ref
    j = pl.program_id(1)

    @pl.when(j == 0)
    def _():
        _proj_row_prologue(x_ref, xb_ref, wa_ref[...], aup_ref, ab_ref, la_ref)

    h_ref[...] = _dot(xb_ref[...], wm_ref[...])


def _proj(x, w_in, aup, abias, layer, tm, tn, seg):
    T, D = x.shape
    GK = aup.shape[2]
    head, mid, rank, tail = seg
    lo0 = head + mid
    NH = lo0 + tail
    tn0 = tn // 2
    assert lo0 % tn0 == 0 and head % tn0 == 0 and tail % tn0 == 0 and lo0 % V7X_LANES == 0
    n_shift, n_al = tail // tn0, lo0 // tn0
    per_tile = tn0 // V7X_LANES

    def src_block(j):
        return jnp.where(j < n_shift, n_al + j, (j - n_shift + head // tn0) % n_al)

    def next_block(j):
        return (n_al + jnp.minimum(j, n_shift - 1) + 1) * per_tile

    const = lambda i, j: (layer, 0, 0)
    h0, la0, wm, wa = pl.pallas_call(
        functools.partial(_proj_first_kernel, n_shift=n_shift, rank=rank),
        grid=(1, NH // tn0),
        in_specs=[
            pl.BlockSpec((tm, D), lambda i, j: (0, 0), pipeline_mode=pl.Buffered(1)),
            pl.BlockSpec((None, D, tn0), lambda i, j: (layer, 0, src_block(j))),
            pl.BlockSpec((None, D, V7X_LANES), lambda i, j: (layer, 0, next_block(j))),
            pl.BlockSpec((None, D, V7X_LANES), lambda i, j: (layer, 0, lo0 // V7X_LANES)),
            pl.BlockSpec((None, V7X_LANES, GK), const),
            pl.BlockSpec((None, 1, GK), const),
        ],
        out_specs=[
            pl.BlockSpec((tm, tn0), lambda i, j: (0, j)),
            pl.BlockSpec((tm, GK), lambda i, j: (0, 0)),
            pl.BlockSpec((D, tn0), lambda i, j: (0, j)),
            pl.BlockSpec((D, V7X_LANES), lambda i, j: (0, 0)),
        ],
        out_shape=[
            jax.ShapeDtypeStruct((T, NH), F32), jax.ShapeDtypeStruct((T, GK), F32),
            jax.ShapeDtypeStruct((D, NH), BF16), jax.ShapeDtypeStruct((D, V7X_LANES), BF16),
        ],
        scratch_shapes=[pltpu.VMEM((tm, D), BF16)],
        compiler_params=_cparams(("arbitrary", "arbitrary")),
        name="proj_first",
    )(x, w_in, w_in, w_in, aup, abias)

    return pl.pallas_call(
        _proj_kernel,
        grid=(T // tm - 1, NH // tn),
        in_specs=[
            pl.BlockSpec((tm, D), lambda i, j: (i + 1, 0)),
            pl.BlockSpec((D, tn), lambda i, j: (0, j)),
            pl.BlockSpec((D, V7X_LANES), lambda i, j: (0, 0)),
            pl.BlockSpec((None, V7X_LANES, GK), const),
            pl.BlockSpec((None, 1, GK), const),
            pl.BlockSpec(memory_space=pl.ANY),
            pl.BlockSpec(memory_space=pl.ANY),
        ],
        out_specs=[
            pl.BlockSpec((tm, tn), lambda i, j: (i + 1, j)),
            pl.BlockSpec((tm, GK), lambda i, j: (i + 1, 0)),
        ],
        out_shape=[jax.ShapeDtypeStruct((T, NH), F32), jax.ShapeDtypeStruct((T, GK), F32)],
        input_output_aliases={5: 0, 6: 1},
        scratch_shapes=[pltpu.VMEM((tm, D), BF16)],
        compiler_params=_cparams(("parallel", "arbitrary")),
        name="proj",
    )(x, wm, wa, aup, abias, h0, la0)


def _stack01(mats, dtype):
    return jnp.asarray(np.concatenate(mats, axis=0).astype(np.float32), dtype=dtype)


def _cumsum_mat(rows):
    i = np.arange(rows)[:, None]
    j = np.arange(rows)[None, :]
    return _stack01([j <= i], BF16)


def _seq_cumsum_mats(rows, seq):
    i = np.arange(rows)[:, None]
    j = np.arange(rows)[None, :]
    same = i // seq == j // seq
    return _stack01([same & (j <= i), same], BF16)


def _level_masks(rows, groups):
    i = np.arange(rows)[:, None]
    j = np.arange(rows)[None, :]
    out = []
    for grp in groups:
        half = grp // 2
        out.append((i // grp == j // grp) & (i % grp >= half) & (j % grp < half))
    return _stack01(out, F32)


def _intra_chunk_scores(qs, k, b, groups, lmask_ref, sub):
    C = qs.shape[0]
    A = jnp.zeros((C, C), F32)
    for n, grp in enumerate(groups):
        e = jnp.concatenate(
            [jnp.exp2(-jnp.abs(b[s:s + grp] - b[s + grp // 2 - 1:s + grp // 2])) for s in range(0, C, grp)],
            axis=0)
        A = A + _dot_nt((qs * e).astype(BF16), (k * e).astype(BF16)) * lmask_ref[n * C:(n + 1) * C, :]
    row_in_sub = lax.broadcasted_iota(jnp.int32, (C, 1), 0) & (sub - 1)
    delta = lax.broadcasted_iota(jnp.int32, (C, C), 0) - lax.broadcasted_iota(jnp.int32, (C, C), 1)
    for d in range(sub):
        if d == 0:
            t = qs * k
        else:
            kd = pltpu.roll(k, d, axis=0)
            bd = pltpu.roll(b, d, axis=0)
            t = qs * kd * jnp.exp2(jnp.minimum(b - bd, 0.0))
        red = jnp.sum(t, axis=1, keepdims=True)
        red = jnp.where(row_in_sub >= d, red, 0.0)
        A = A + jnp.where(delta == d, red, 0.0)
    return A


def _head_norm(o):
    return o * lax.rsqrt(jnp.mean(o * o, axis=-1, keepdims=True) + HEAD_NORM_EPS)


def _gla_prompt_kernel(q_ref, k_ref, v_ref, la_ref, cmat_ref, lmask_ref, o_ref, sout_ref, s_ref,
                       *, scale, sub, groups):
    t = pl.program_id(2)
    C = q_ref.shape[0]
    HB, DK, DV = s_ref.shape

    @pl.when(t == 0)
    def _():
        s_ref[...] = jnp.zeros_like(s_ref)

    cm = cmat_ref[...]
    p0, p1, p2 = _split3(la_ref[...] * LOG2_E)
    b_all = _dot(cm, p0) + _dot(cm, p1) + _dot(cm, p2)
    for hh in range(HB):
        kc = slice(hh * DK, (hh + 1) * DK)
        vc = slice(hh * DV, (hh + 1) * DV)
        qs = q_ref[:, kc] * scale
        k = k_ref[:, kc]
        vb = v_ref[:, vc].astype(BF16)
        b = b_all[:, kc]

        A = _intra_chunk_scores(qs, k, b, groups, lmask_ref, sub)
        S = s_ref[hh]
        o = _dot((qs * jnp.exp2(b)).astype(BF16), S.astype(BF16)) + _dot(A.astype(BF16), vb)
        o_ref[:, vc] = _head_norm(o)

        bl = b[C - 1:C, :]
        ku_t = (k * jnp.exp2(bl - b)).T
        d_s = _dot(ku_t.astype(BF16), vb)
        a_t = jnp.exp2(jnp.broadcast_to(bl, (V7X_LANES, DK))).T
        for n in range(DV // V7X_LANES):
            sl = slice(n * V7X_LANES, (n + 1) * V7X_LANES)
            s_ref[hh, :, sl] = a_t * S[:, sl] + d_s[:, sl]

    @pl.when(t == pl.num_programs(2) - 1)
    def _():
        sout_ref[...] = s_ref[...]


def _gla_prompt(h, la, batch, seq, heads, dk, dv, q_off, k_off, v_off):
    C = GLA_CHUNK_ROWS
    HB = GLA_HEADS_PER_STEP
    nt = seq // C
    groups = []
    g = C
    while g > GLA_SUB:
        groups.append(g)
        g //= 2
    cmat = _cumsum_mat(C)
    lmask = _level_masks(C, groups)
    wk, wv = HB * dk, HB * dv
    assert heads % HB == 0 and q_off % wk == 0 and k_off % wk == 0 and v_off % wv == 0
    qb, kb, vb = q_off // wk, k_off // wk, v_off // wv
    return pl.pallas_call(
        functools.partial(_gla_prompt_kernel, scale=float(dk) ** -0.5, sub=GLA_SUB,
                          groups=tuple(groups)),
        grid=(batch, heads // HB, nt),
        in_specs=[
            pl.BlockSpec((C, wk), lambda bi, hi, t: (bi * nt + t, qb + hi)),
            pl.BlockSpec((C, wk), lambda bi, hi, t: (bi * nt + t, kb + hi)),
            pl.BlockSpec((C, wv), lambda bi, hi, t: (bi * nt + t, vb + hi)),
            pl.BlockSpec((C, wk), lambda bi, hi, t: (bi * nt + t, hi)),
            pl.BlockSpec(cmat.shape, lambda bi, hi, t: (0, 0)),
            pl.BlockSpec(lmask.shape, lambda bi, hi, t: (0, 0)),
        ],
        out_specs=[
            pl.BlockSpec((C, wv), lambda bi, hi, t: (bi * nt + t, hi)),
            pl.BlockSpec((None, HB, dk, dv), lambda bi, hi, t: (bi, hi, 0, 0)),
        ],
        out_shape=[
            jax.ShapeDtypeStruct((batch * seq, heads * dv), F32),
            jax.ShapeDtypeStruct((batch, heads, dk, dv), F32),
        ],
        scratch_shapes=[pltpu.VMEM((HB, dk, dv), F32)],
        compiler_params=_cparams(("parallel", "parallel", "arbitrary")),
        name="gla_prompt",
    )(h, h, h, la, cmat, lmask)


def _gla_sample_kernel(q_ref, k_ref, v_ref, la_ref, cmat_ref, sin_ref, *rest, scale, seq):
    o_ref, sout_ref = rest[-2:]
    R, DK = q_ref.shape
    DV = v_ref.shape[1]
    nb = R // seq

    qs = q_ref[...] * scale
    k = k_ref[...]
    vb = v_ref[...].astype(BF16)
    cm = cmat_ref[...]
    p0, p1, p2 = _split3(la_ref[...] * LOG2_E)
    ball = _dot(cm, p0) + _dot(cm, p1) + _dot(cm, p2)
    b = ball[0:R]
    bl = ball[R:2 * R]

    A = _intra_chunk_scores(qs, k, b, (), None, seq)
    o_intra = _dot(A.astype(BF16), vb)
    qo = qs * jnp.exp2(b)
    ku_t = (k * jnp.exp2(bl - b)).T
    lane_seq = lax.broadcasted_iota(jnp.int32, (DK, R), 1) // seq
    for i in range(nb):
        rows = slice(i * seq, (i + 1) * seq)
        S = sin_ref[i]
        o = _dot(qo[rows].astype(BF16), S.astype(BF16)) + o_intra[rows]
        o_ref[rows, :] = _head_norm(o)
        d_s = _dot(jnp.where(lane_seq == i, ku_t, 0.0).astype(BF16), vb)
        a_t = jnp.exp2(jnp.broadcast_to(bl[i * seq:i * seq + 1, :], (V7X_LANES, DK))).T
        for n in range(DV // V7X_LANES):
            sl = slice(n * V7X_LANES, (n + 1) * V7X_LANES)
            sout_ref[i, :, sl] = a_t * S[:, sl] + d_s[:, sl]


def _gla_sample(h, la, state, new_state, layer, row0, batch, seq, heads, dk, dv, q_off, k_off, v_off):
    R = V7X_LANES
    nb = R // seq
    ng = batch // nb
    rb = row0 // R
    cmat = _seq_cumsum_mats(R, seq)
    qb, kb, vb = q_off // dk, k_off // dk, v_off // dv
    carried = [] if new_state is None else [new_state]
    return pl.pallas_call(
        functools.partial(_gla_sample_kernel, scale=float(dk) ** -0.5, seq=seq),
        grid=(ng, heads),
        in_specs=[
            pl.BlockSpec((R, dk), lambda g, hi: (rb + g, qb + hi)),
            pl.BlockSpec((R, dk), lambda g, hi: (rb + g, kb + hi)),
            pl.BlockSpec((R, dv), lambda g, hi: (rb + g, vb + hi)),
            pl.BlockSpec((R, dk), lambda g, hi: (rb + g, hi)),
            pl.BlockSpec(cmat.shape, lambda g, hi: (0, 0)),
            pl.BlockSpec((None, nb, None, dk, dv), lambda g, hi: (layer, g, hi, 0, 0)),
        ] + [pl.BlockSpec(memory_space=pl.ANY)] * len(carried),
        out_specs=[
            pl.BlockSpec((R, dv), lambda g, hi: (g, hi)),
            pl.BlockSpec((None, nb, None, dk, dv), lambda g, hi: (layer, g, hi, 0, 0)),
        ],
        out_shape=[
            jax.ShapeDtypeStruct((batch * seq, heads * dv), F32),
            jax.ShapeDtypeStruct(state.shape, F32),
        ],
        input_output_aliases={6: 1} if carried else {},
        compiler_params=_cparams(("parallel", "parallel")),
        name="gla_sample",
    )(h, h, h, la, cmat, state, *carried)


def _window_means_minus_input(ext, u, pos, lead):
    n, pw = u.shape
    gw = pw // len(POOL_WINDOWS)
    outs = []
    for g, w in enumerate(POOL_WINDOWS):
        cols = slice(g * gw, (g + 1) * gw)
        p = ext[:, cols]
        span = 1
        while span < w:
            p = p + pltpu.roll(p, span, axis=0)
            span *= 2
        cnt = jnp.minimum(w, pos + 1).astype(F32)
        outs.append(p[lead:lead + n] / cnt - u[:, cols])
    return outs


def _mix_tail(pooled, on, r, ga, gb, x, hg_ref, wgo_ref, pw_ref, ps_ref, wo_ref, g_ref, b_ref, alpha):
    ya = jnp.concatenate(
        [_dot(p.astype(BF16), pw_ref[g]) for g, p in enumerate(pooled)], axis=1) * ps_ref[...]
    yb = _dot((on * hg_ref[...] * (r * jax.nn.sigmoid(r))).astype(BF16), wgo_ref[...])
    merged = jax.nn.sigmoid(ga) * ya + jax.nn.sigmoid(gb) * yb
    m = _dot(merged.astype(BF16), wo_ref[...])
    return _layer_norm(alpha * x + m, g_ref[...], b_ref[...])


def _post_prompt_kernel(on_ref, r_ref, ga_ref, gb_ref, u_ref, x_ref, hg_ref, wgo_ref, pw_ref, ps_ref,
                        wo_ref, g_ref, b_ref, o_ref, npool_ref, halo_ref, *, tiles_per_seq, alpha):
    i = pl.program_id(0)
    tm = u_ref.shape[0]
    lead = halo_ref.shape[0]
    ti = i % tiles_per_seq

    @pl.when(ti == 0)
    def _():
        halo_ref[...] = jnp.zeros_like(halo_ref)

    u = u_ref[...]
    ext = jnp.concatenate([halo_ref[...], u], axis=0)
    halo_ref[...] = u[tm - lead:, :]

    @pl.when(ti == tiles_per_seq - 1)
    def _():
        npool_ref[...] = u[tm - POOL_BUF:, :]

    pos = ti * tm + lax.broadcasted_iota(jnp.int32, (tm, 1), 0)
    pooled = _window_means_minus_input(ext, u, pos, lead)
    o_ref[...] = _mix_tail(pooled, on_ref[...], r_ref[...], ga_ref[...], gb_ref[...], x_ref[...],
                           hg_ref, wgo_ref, pw_ref, ps_ref, wo_ref, g_ref, b_ref, alpha)


def _post_sample_kernel(on_ref, r_ref, ga_ref, gb_ref, u_ref, x_ref, st_ref, hg_ref, wgo_ref, pw_ref,
                        ps_ref, wo_ref, g_ref, b_ref, carried_ref, o_ref, npool_ref, ext_ref,
                        *, seq, start, alpha):
    del carried_ref
    tm, pw = u_ref.shape
    nb = tm // seq
    lead = ext_ref.shape[1] - seq
    u = u_ref[...]
    u3 = u.reshape(nb, seq, pw)
    st = st_ref[...]
    ext_ref[:, 0:lead - POOL_BUF, :] = jnp.zeros((nb, lead - POOL_BUF, pw), F32)
    ext_ref[:, lead - POOL_BUF:lead, :] = st
    ext_ref[:, lead:, :] = u3
    e3 = ext_ref[...]
    npool_ref[...] = e3[:, lead + seq - POOL_BUF:, :]
    p = e3.reshape(nb * (lead + seq), pw)
    gw = pw // len(POOL_WINDOWS)
    pos = start + lax.broadcasted_iota(jnp.int32, (nb, seq, 1), 1)
    pooled = []
    for g, w in enumerate(POOL_WINDOWS):
        cols = slice(g * gw, (g + 1) * gw)
        pg = p[:, cols]
        span = 1
        while span < w:
            pg = pg + pltpu.roll(pg, span, axis=0)
            span *= 2
        win = pg.reshape(nb, lead + seq, gw)[:, lead:, :]
        cnt = jnp.minimum(w, pos + 1).astype(F32)
        pooled.append((win / cnt - u3[:, :, cols]).reshape(tm, gw))
    o_ref[...] = _mix_tail(pooled, on_ref[...], r_ref[...], ga_ref[...], gb_ref[...], x_ref[...],
                           hg_ref, wgo_ref, pw_ref, ps_ref, wo_ref, g_ref, b_ref, alpha)


def _const_spec(shape, index_map):
    return pl.BlockSpec(shape, index_map, pipeline_mode=pl.Buffered(1))


def _post_weight_specs(layer, ln_idx, D, GV, groups, gw, ogw, n):
    z = (0,) * (n - 1)
    return [
        _const_spec((None, 1, GV), lambda *a: (layer, 0, 0)),
        _const_spec((None, GV, D), lambda *a: (layer, 0, 0)),
        _const_spec((None, groups, gw, ogw), lambda *a: (layer, 0, 0, 0)),
        _const_spec((None, 1, D), lambda *a: (layer, 0, 0)),
        _const_spec((None, D, D), lambda *a: (layer, 0, 0)),
        _const_spec((None, None, 1, D), lambda *a: (layer, ln_idx, 0, 0)),
        _const_spec((None, None, 1, D), lambda *a: (layer, ln_idx, 0, 0)),
    ]


def _post_prompt(on, h, x, weights, layer, ln_idx, alpha, seq, tm, cols):
    TP, GV = on.shape
    T, D = x.shape
    hg, wgo, pw, ps, wo, ln_g, ln_b = weights
    groups, gw, ogw = pw.shape[1:]
    PW = groups * gw
    tiles_per_seq = seq // tm
    r_b, ga_b, gb_b, u_b = cols["r"] // GV, cols["ga"] // D, cols["gb"] // D, cols["u"] // PW
    lead = 2 * V7X_SUBLANES
    return pl.pallas_call(
        functools.partial(_post_prompt_kernel, tiles_per_seq=tiles_per_seq, alpha=alpha),
        grid=(TP // tm,),
        in_specs=[
            pl.BlockSpec((tm, GV), lambda i: (i, 0)),
            pl.BlockSpec((tm, GV), lambda i: (i, r_b)),
            pl.BlockSpec((tm, D), lambda i: (i, ga_b)),
            pl.BlockSpec((tm, D), lambda i: (i, gb_b)),
            pl.BlockSpec((tm, PW), lambda i: (i, u_b)),
            pl.BlockSpec((tm, D), lambda i: (i, 0)),
        ] + _post_weight_specs(layer, ln_idx, D, GV, groups, gw, ogw, 1),
        out_specs=[
            pl.BlockSpec((tm, D), lambda i: (i, 0)),
            pl.BlockSpec((None, POOL_BUF, PW), lambda i: (i // tiles_per_seq, 0, 0)),
        ],
        out_shape=[
            jax.ShapeDtypeStruct((T, D), F32),
            jax.ShapeDtypeStruct((TP // seq, POOL_BUF, PW), F32),
        ],
        scratch_shapes=[pltpu.VMEM((lead, PW), F32)],
        compiler_params=_cparams(("arbitrary",)),
        name="post_prompt",
    )(on, h, h, h, h, x, hg, wgo, pw, ps, wo, ln_g, ln_b)


def _post_sample(on, h, x, x2, state_pool, weights, layer, ln_idx, alpha, row0, seq, tm, cols):
    TS, GV = on.shape
    D = x.shape[1]
    hg, wgo, pw, ps, wo, ln_g, ln_b = weights
    groups, gw, ogw = pw.shape[1:]
    PW = groups * gw
    nb = tm // seq
    batch = TS // seq
    rb = row0 // tm
    r_b, ga_b, gb_b, u_b = cols["r"] // GV, cols["ga"] // D, cols["gb"] // D, cols["u"] // PW
    lead = 2 * V7X_SUBLANES
    return pl.pallas_call(
        functools.partial(_post_sample_kernel, seq=seq, start=PAST_LEN, alpha=alpha),
        grid=(TS // tm,),
        in_specs=[
            pl.BlockSpec((tm, GV), lambda i: (i, 0)),
            pl.BlockSpec((tm, GV), lambda i: (rb + i, r_b)),
            pl.BlockSpec((tm, D), lambda i: (rb + i, ga_b)),
            pl.BlockSpec((tm, D), lambda i: (rb + i, gb_b)),
            pl.BlockSpec((tm, PW), lambda i: (rb + i, u_b)),
            pl.BlockSpec((tm, D), lambda i: (rb + i, 0)),
            pl.BlockSpec((None, nb, POOL_BUF, PW), lambda i: (layer, i, 0, 0)),
        ] + _post_weight_specs(layer, ln_idx, D, GV, groups, gw, ogw, 1) + [
            pl.BlockSpec(memory_space=pl.ANY),
        ],
        out_specs=[
            pl.BlockSpec((tm, D), lambda i: (rb + i, 0)),
            pl.BlockSpec((nb, POOL_BUF, PW), lambda i: (i, 0, 0)),
        ],
        out_shape=[
            jax.ShapeDtypeStruct(x2.shape, F32),
            jax.ShapeDtypeStruct((batch, POOL_BUF, PW), F32),
        ],
        input_output_aliases={14: 0},
        scratch_shapes=[pltpu.VMEM((nb, lead + seq, PW), F32)],
        compiler_params=_cparams(("arbitrary",)),
        name="post_sample",
    )(on, h, h, h, h, x, state_pool, hg, wgo, pw, ps, wo, ln_g, ln_b, x2)


def _tiles(T, TP, seq_p, F, NH):
    def fit(n, t):
        while n % t:
            t //= 2
        return t
    return dict(
        ffn_tm=fit(np.gcd(T, TP), 1024), ffn_tf=fit(F, 512),
        proj_tm=fit(np.gcd(T, TP), 1024), proj_tn=fit(NH, 1024),
        post_tm=fit(seq_p, 256), post_sample_tm=fit(np.gcd(T - TP, TP), 128),
    )


def kernel(x_prompt, x_sample, state_pool, state_gla, ln_g, ln_b, w_ffn_in, w_ffn_out, w_in, pool_w,
           pool_scale, a_up, a_bias, head_g, w_gla_out, w_out):
    B, SEQ, D = x_prompt.shape
    DB, DSEQ, _ = x_sample.shape
    L = w_in.shape[0]
    PW = state_pool.shape[-1]
    H, DK, DV = state_gla.shape[2:]
    GK, GV = H * DK, H * DV
    RANK = a_up.shape[1]
    F = w_ffn_out.shape[2]
    TP, TS = B * SEQ, DB * DSEQ
    T = TP + TS
    alpha = float((2 * L) ** 0.25)
    assert SEQ % GLA_CHUNK_ROWS == 0 and TP % V7X_LANES == 0 and V7X_LANES % DSEQ == 0
    assert DSEQ == V7X_SUBLANES and PAST_LEN >= POOL_BUF

    seg = (PW + 2 * GK, GV, RANK, GV + 2 * D)
    assert sum(seg) == w_in.shape[2]
    w_in_b = w_in.astype(BF16)
    aup = jnp.pad(a_up, ((0, 0), (0, V7X_LANES - RANK), (0, 0))).astype(BF16)
    abias = a_bias.reshape(L, 1, GK)
    ln_g4 = ln_g.reshape(L, 3, 1, D)
    ln_b4 = ln_b.reshape(L, 3, 1, D)
    post_w = (head_g.reshape(L, 1, GV), w_gla_out.astype(BF16), pool_w.astype(BF16),
              pool_scale.reshape(L, 1, D), w_out.astype(BF16), ln_g4, ln_b4)
    c_v = GV + 2 * D
    c_u = c_v + GV
    cols = dict(r=0, ga=GV, gb=GV + D, v=c_v, u=c_u, q=c_u + PW, k=c_u + PW + GK)
    ts = _tiles(T, TP, SEQ, F, w_in.shape[2] - RANK)
    assert GV == D and c_v % DV == 0 and c_u % PW == 0 and cols["q"] % DK == 0

    ffn = functools.partial(_ffn_ln, w_ffn_in=w_ffn_in, w_ffn_out=w_ffn_out, ln_g=ln_g4, ln_b=ln_b4,
                            alpha=alpha, tm=ts["ffn_tm"], tf=ts["ffn_tf"])

    xs = [x_prompt.reshape(TP, D), x_sample.reshape(TS, D)]
    pool_p, gla_p, pool_s, gla_s = [], [], [], None
    for l in range(L):
        (x,) = ffn(xs, [T], layer=l, which=0, ln_idx=0)
        h, la = _proj(x, w_in_b, aup, abias, l, ts["proj_tm"], ts["proj_tn"], seg)
        on_p, s_p = _gla_prompt(h, la, B, SEQ, H, DK, DV, cols["q"], cols["k"], cols["v"])
        on_s, gla_s = _gla_sample(h, la, state_gla, gla_s, l, TP, DB, DSEQ, H, DK, DV,
                                  cols["q"], cols["k"], cols["v"])
        x2, np_p = _post_prompt(on_p, h, x, post_w, l, 1, alpha, SEQ, ts["post_tm"], cols)
        x2, np_s = _post_sample(on_s, h, x, x2, state_pool, post_w, l, 1, alpha, TP, DSEQ,
                                ts["post_sample_tm"], cols)
        xs = ffn([x2], [T] if l + 1 < L else [TP, TS], layer=l, which=1, ln_idx=2)
        pool_p.append(np_p)
        gla_p.append(s_p)
        pool_s.append(np_s)
    return (xs[0].reshape(B, SEQ, D), xs[1].reshape(DB, DSEQ, D), jnp.stack(pool_p),
            jnp.stack(gla_p), jnp.stack(pool_s), gla_s)
```

```python
import functools

import numpy as np
import jax
import jax.numpy as jnp
from jax import lax
from jax.experimental import pallas as pl
from jax.experimental.pallas import tpu as pltpu

POOL_WINDOWS = (2, 4, 8, 16)
POOL_BUF = max(POOL_WINDOWS) - 1
GLA_TAU = 16.0
LN_EPS = 1e-5
HEAD_NORM_EPS = 1e-6
PAST_LEN = 16384

V7X_LANES = 128
V7X_SUBLANES = 8
V7X_VMEM_BYTES = 64 * 1024 * 1024
V7X_VMEM_LIMIT_BYTES = V7X_VMEM_BYTES - 1024 * 1024

F32 = jnp.float32
BF16 = jnp.bfloat16

GLA_SUB = 4
GLA_CHUNK_ROWS = 128
GLA_SAMPLE_ROWS = 64
LOG2_E = 1.4426950408889634


def _cparams(sem):
    return pltpu.CompilerParams(dimension_semantics=sem, vmem_limit_bytes=V7X_VMEM_LIMIT_BYTES)


def _layer_norm(y, g, b):
    mu = jnp.mean(y, axis=-1, keepdims=True)
    yc = y - mu
    var = jnp.mean(yc * yc, axis=-1, keepdims=True)
    return yc * lax.rsqrt(var + LN_EPS) * g + b


def _dot(a, b):
    return jnp.dot(a, b, preferred_element_type=F32)


def _dot_nt(a, b):
    return lax.dot_general(a, b, (((1,), (1,)), ((), ())), preferred_element_type=F32)


def _split3(x):
    p0 = x.astype(BF16)
    r1 = x - p0.astype(F32)
    p1 = r1.astype(BF16)
    p2 = (r1 - p1.astype(F32)).astype(BF16)
    return p0, p1, p2


FFN_DOWN_CHUNKS = 4


def _ffn_ln_kernel(*refs, alpha, n_carried, cast_weights):
    refs = list(refs)
    x_ref, wg_ref, wu_ref, wo_ref, g_ref, b_ref = [refs.pop(0) for _ in range(6)]
    del refs[:n_carried]
    o_ref = refs.pop(0)
    wb_refs = [refs.pop(0) for _ in range(3)] if cast_weights else None
    (xb_ref,) = refs
    j = pl.program_id(1)

    @pl.when(j == 0)
    def _():
        xb_ref[...] = x_ref[...].astype(BF16)
        o_ref[...] = jnp.zeros_like(o_ref)

    wg, wu, wo = wg_ref[...], wu_ref[...], wo_ref[...]
    if cast_weights:
        wg, wu, wo = wg.astype(BF16), wu.astype(BF16), wo.astype(BF16)
        for w_ref, w in zip(wb_refs, (wg, wu, wo)):
            w_ref[...] = w
    xb = xb_ref[...]
    gate = _dot(xb, wg)
    up = _dot(xb, wu)
    act = (gate * jax.nn.sigmoid(gate) * up).astype(BF16)
    cw = o_ref.shape[1] // FFN_DOWN_CHUNKS
    for c in range(FFN_DOWN_CHUNKS):
        cs = slice(c * cw, (c + 1) * cw)
        o_ref[:, cs] += _dot(act, wo[:, cs])

    @pl.when(j == pl.num_programs(1) - 1)
    def _():
        y = alpha * x_ref[...] + 0.5 * o_ref[...]
        o_ref[...] = _layer_norm(y, g_ref[...], b_ref[...])


def _ffn_ln(xs, out_rows, w_ffn_in, w_ffn_out, ln_g, ln_b, layer, which, ln_idx, alpha, tm, tf):
    D = xs[0].shape[1]
    F = w_ffn_out.shape[2]
    ln_specs = [pl.BlockSpec((None, None, 1, D), lambda i, j: (layer, ln_idx, 0, 0))] * 2
    xb_scratch = pltpu.VMEM((tm, D), BF16)

    def tile_owner(rows_list):
        return [(a, t) for a, rows in enumerate(rows_list) for t in range(rows // tm)]
    runs = []
    for (sa, st), (da, dt) in zip(tile_owner([x.shape[0] for x in xs]), tile_owner(out_rows)):
        if runs and runs[-1][0] == sa and runs[-1][2] == da:
            runs[-1][4] += 1
        else:
            runs.append([sa, st, da, dt, 1])

    tf0 = tf // 2
    nf0 = F // tf0
    sa, st, da, dt, _ = runs[0]
    y0, wg_b, wu_b, wo_b = pl.pallas_call(
        functools.partial(_ffn_ln_kernel, alpha=alpha, n_carried=0, cast_weights=True),
        grid=(1, nf0),
        in_specs=[
            pl.BlockSpec((tm, D), lambda i, j: (st, 0), pipeline_mode=pl.Buffered(1)),
            pl.BlockSpec((None, None, D, tf0), lambda i, j: (layer, which, 0, j)),
            pl.BlockSpec((None, None, D, tf0), lambda i, j: (layer, which, 0, j + nf0)),
            pl.BlockSpec((None, None, tf0, D), lambda i, j: (layer, which, j, 0)),
        ] + ln_specs,
        out_specs=[
            pl.BlockSpec((tm, D), lambda i, j: (dt, 0)),
            pl.BlockSpec((D, tf0), lambda i, j: (0, j)),
            pl.BlockSpec((D, tf0), lambda i, j: (0, j)),
            pl.BlockSpec((tf0, D), lambda i, j: (j, 0)),
        ],
        out_shape=[
            jax.ShapeDtypeStruct((out_rows[da], D), F32),
            jax.ShapeDtypeStruct((D, F), BF16),
            jax.ShapeDtypeStruct((D, F), BF16),
            jax.ShapeDtypeStruct((F, D), BF16),
        ],
        scratch_shapes=[xb_scratch],
        compiler_params=_cparams(("arbitrary", "arbitrary")),
        name="ffn_ln_first",
    )(xs[sa], w_ffn_in, w_ffn_in, w_ffn_out, ln_g, ln_b)
    outs = {da: y0}
    runs[0][1] += 1
    runs[0][3] += 1
    runs[0][4] -= 1

    nf = F // tf
    for sa, st, da, dt, n in runs:
        if n == 0:
            continue
        carried = [outs[da]] if da in outs else []
        outs[da] = pl.pallas_call(
            functools.partial(_ffn_ln_kernel, alpha=alpha, n_carried=len(carried), cast_weights=False),
            grid=(n, nf),
            in_specs=[
                pl.BlockSpec((tm, D), lambda i, j, st=st: (i + st, 0)),
                pl.BlockSpec((D, tf), lambda i, j: (0, j)),
                pl.BlockSpec((D, tf), lambda i, j: (0, j)),
                pl.BlockSpec((tf, D), lambda i, j: (j, 0)),
            ] + ln_specs + [pl.BlockSpec(memory_space=pl.ANY)] * len(carried),
            out_specs=pl.BlockSpec((tm, D), lambda i, j, dt=dt: (i + dt, 0)),
            out_shape=jax.ShapeDtypeStruct((out_rows[da], D), F32),
            input_output_aliases={6: 0} if carried else {},
            scratch_shapes=[xb_scratch],
            compiler_params=_cparams(("arbitrary", "arbitrary")),
            name="ffn_ln",
        )(xs[sa], wg_b, wu_b, wo_b, ln_g, ln_b, *carried)
    return [outs[a] for a in range(len(out_rows))]


def _proj_row_prologue(x_ref, xb_ref, wa, aup_ref, ab_ref, la_ref):
    xb = x_ref[...].astype(BF16)
    xb_ref[...] = xb
    a_lo = _dot(xb, wa)
    z = _dot(a_lo.astype(BF16), aup_ref[...]) + ab_ref[...]
    la_ref[...] = (jnp.minimum(z, 0.0) - jnp.log(1.0 + jnp.exp(-jnp.abs(z)))) / GLA_TAU


def _proj_first_kernel(x_ref, wsrc_ref, wnext_ref, walo_ref, aup_ref, ab_ref, h_ref, la_ref, wm_ref,
                       wa_ref, xb_ref, *, n_shift, rank):
    j = pl.program_id(1)
    tn = wsrc_ref.shape[1]

    @pl.when(j == 0)
    def _():
        lane = lax.broadcasted_iota(jnp.int32, walo_ref.shape, 1)
        wa = jnp.where(lane < rank, walo_ref[...].astype(F32), 0.0).astype(BF16)
        wa_ref[...] = wa
        _proj_row_prologue(x_ref, xb_ref, wa, aup_ref, ab_ref, la_ref)

    def emit(w):
        wm_ref[...] = w
        h_ref[...] = _dot(xb_ref[...], w)

    @pl.when(j < n_shift)
    def _():
        both = jnp.concatenate([wsrc_ref[...], wnext_ref[...]], axis=1).astype(F32)
        emit(pltpu.roll(both, both.shape[1] - rank, axis=1)[:, :tn].astype(BF16))

    @pl.when(j >= n_shift)
    def _():
        emit(wsrc_ref[...])


def _proj_kernel(x_ref, wm_ref, wa_ref, aup_ref, ab_ref, h_in_ref, la_in_ref, h_ref, la_ref, xb_ref):
    del h_in_ref, la_in_ref
    j = pl.program_id(1)

    @pl.when(j == 0)
    def _():
        _proj_row_prologue(x_ref, xb_ref, wa_ref[...], aup_ref, ab_ref, la_ref)

    h_ref[...] = _dot(xb_ref[...], wm_ref[...])


def _proj(x, w_in, aup, abias, layer, tm, tn, seg):
    T, D = x.shape
    GK = aup.shape[2]
    head, mid, rank, tail = seg
    lo0 = head + mid
    NH = lo0 + tail
    tn0 = tn // 2
    assert lo0 % tn0 == 0 and head % tn0 == 0 and tail % tn0 == 0 and lo0 % V7X_LANES == 0
    n_shift, n_al = tail // tn0, lo0 // tn0
    per_tile = tn0 // V7X_LANES

    def src_block(j):
        return jnp.where(j < n_shift, n_al + j, (j - n_shift + head // tn0) % n_al)

    def next_block(j):
        return (n_al + jnp.minimum(j, n_shift - 1) + 1) * per_tile

    const = lambda i, j: (layer, 0, 0)
    h0, la0, wm, wa = pl.pallas_call(
        functools.partial(_proj_first_kernel, n_shift=n_shift, rank=rank),
        grid=(1, NH // tn0),
        in_specs=[
            pl.BlockSpec((tm, D), lambda i, j: (0, 0), pipeline_mode=pl.Buffered(1)),
            pl.BlockSpec((None, D, tn0), lambda i, j: (layer, 0, src_block(j))),
            pl.BlockSpec((None, D, V7X_LANES), lambda i, j: (layer, 0, next_block(j))),
            pl.BlockSpec((None, D, V7X_LANES), lambda i, j: (layer, 0, lo0 // V7X_LANES)),
            pl.BlockSpec((None, V7X_LANES, GK), const),
            pl.BlockSpec((None, 1, GK), const),
        ],
        out_specs=[
            pl.BlockSpec((tm, tn0), lambda i, j: (0, j)),
            pl.BlockSpec((tm, GK), lambda i, j: (0, 0)),
            pl.BlockSpec((D, tn0), lambda i, j: (0, j)),
            pl.BlockSpec((D, V7X_LANES), lambda i, j: (0, 0)),
        ],
        out_shape=[
            jax.ShapeDtypeStruct((T, NH), F32), jax.ShapeDtypeStruct((T, GK), F32),
            jax.ShapeDtypeStruct((D, NH), BF16), jax.ShapeDtypeStruct((D, V7X_LANES), BF16),
        ],
        scratch_shapes=[pltpu.VMEM((tm, D), BF16)],
        compiler_params=_cparams(("arbitrary", "arbitrary")),
        name="proj_first",
    )(x, w_in, w_in, w_in, aup, abias)

    return pl.pallas_call(
        _proj_kernel,
        grid=(T // tm - 1, NH // tn),
        in_specs=[
            pl.BlockSpec((tm, D), lambda i, j: (i + 1, 0)),
            pl.BlockSpec((D, tn), lambda i, j: (0, j)),
            pl.BlockSpec((D, V7X_LANES), lambda i, j: (0, 0)),
            pl.BlockSpec((None, V7X_LANES, GK), const),
            pl.BlockSpec((None, 1, GK), const),
            pl.BlockSpec(memory_space=pl.ANY),
            pl.BlockSpec(memory_space=pl.ANY),
        ],
        out_specs=[
            pl.BlockSpec((tm, tn), lambda i, j: (i + 1, j)),
            pl.BlockSpec((tm, GK), lambda i, j: (i + 1, 0)),
        ],
        out_shape=[jax.ShapeDtypeStruct((T, NH), F32), jax.ShapeDtypeStruct((T, GK), F32)],
        input_output_aliases={5: 0, 6: 1},
        scratch_shapes=[pltpu.VMEM((tm, D), BF16)],
        compiler_params=_cparams(("parallel", "arbitrary")),
        name="proj",
    )(x, wm, wa, aup, abias, h0, la0)


def _stack01(mats, dtype):
    return jnp.asarray(np.concatenate(mats, axis=0).astype(np.float32), dtype=dtype)


def _cumsum_mat(rows):
    i = np.arange(rows)[:, None]
    j = np.arange(rows)[None, :]
    return _stack01([j <= i], BF16)


def _seq_cumsum_mats(rows, seq):
    i = np.arange(rows)[:, None]
    j = np.arange(rows)[None, :]
    same = i // seq == j // seq
    return _stack01([same & (j <= i), same], BF16)


def _level_masks(rows, groups):
    i = np.arange(rows)[:, None]
    j = np.arange(rows)[None, :]
    out = []
    for grp in groups:
        half = grp // 2
        out.append((i // grp == j // grp) & (i % grp >= half) & (j % grp < half))
    return _stack01(out, F32)


def _intra_chunk_scores(qs, k, b, groups, lmask_ref, sub):
    C = qs.shape[0]
    A = jnp.zeros((C, C), F32)
    for n, grp in enumerate(groups):
        e = jnp.concatenate(
            [jnp.exp2(-jnp.abs(b[s:s + grp] - b[s + grp // 2 - 1:s + grp // 2])) for s in range(0, C, grp)],
            axis=0)
        A = A + _dot_nt((qs * e).astype(BF16), (k * e).astype(BF16)) * lmask_ref[n * C:(n + 1) * C, :]
    row_in_sub = lax.broadcasted_iota(jnp.int32, (C, 1), 0) & (sub - 1)
    delta = lax.broadcasted_iota(jnp.int32, (C, C), 0) - lax.broadcasted_iota(jnp.int32, (C, C), 1)
    for d in range(sub):
        if d == 0:
            t = qs * k
        else:
            kd = pltpu.roll(k, d, axis=0)
            bd = pltpu.roll(b, d, axis=0)
            t = qs * kd * jnp.exp2(jnp.minimum(b - bd, 0.0))
        red = jnp.sum(t, axis=1, keepdims=True)
        red = jnp.where(row_in_sub >= d, red, 0.0)
        A = A + jnp.where(delta == d, red, 0.0)
    return A


def _head_norm(o):
    return o * lax.rsqrt(jnp.mean(o * o, axis=-1, keepdims=True) + HEAD_NORM_EPS)


def _gla_prompt_step(t, nt, q_ref, k_ref, v_ref, la_ref, cmat_ref, lmask_ref, o_ref, sout_ref, s_ref,
                     *, scale, sub, groups):
    C = q_ref.shape[0]
    HB, DK, DV = s_ref.shape

    @pl.when(t == 0)
    def _():
        s_ref[...] = jnp.zeros_like(s_ref)

    cm = cmat_ref[...]
    p0, p1, p2 = _split3(la_ref[...] * LOG2_E)
    b_all = _dot(cm, p0) + _dot(cm, p1) + _dot(cm, p2)
    for hh in range(HB):
        kc = slice(hh * DK, (hh + 1) * DK)
        vc = slice(hh * DV, (hh + 1) * DV)
        qs = q_ref[:, kc] * scale
        k = k_ref[:, kc]
        vb = v_ref[:, vc].astype(BF16)
        b = b_all[:, kc]

        A = _intra_chunk_scores(qs, k, b, groups, lmask_ref, sub)
        S = s_ref[hh]
        o = _dot((qs * jnp.exp2(b)).astype(BF16), S.astype(BF16)) + _dot(A.astype(BF16), vb)
        o_ref[:, vc] = _head_norm(o)

        bl = b[C - 1:C, :]
        ku_t = (k * jnp.exp2(bl - b)).T
        d_s = _dot(ku_t.astype(BF16), vb)
        a_t = jnp.exp2(jnp.broadcast_to(bl, (V7X_LANES, DK))).T
        for n in range(DV // V7X_LANES):
            sl = slice(n * V7X_LANES, (n + 1) * V7X_LANES)
            s_ref[hh, :, sl] = a_t * S[:, sl] + d_s[:, sl]

    @pl.when(t == nt - 1)
    def _():
        sout_ref[...] = s_ref[...]


def _gla_sample_step(q_ref, k_ref, v_ref, la_ref, cmat_ref, sin_ref, o_ref, sout_ref, *, scale, seq):
    R, DK = q_ref.shape
    DV = v_ref.shape[1]
    nb = R // seq
    pad = V7X_LANES - R

    qs = q_ref[...] * scale
    k = k_ref[...]
    vb = v_ref[...].astype(BF16)
    cm = cmat_ref[...]
    p0, p1, p2 = _split3(la_ref[...] * LOG2_E)
    ball = _dot(cm, p0) + _dot(cm, p1) + _dot(cm, p2)
    b = ball[0:R]
    bl = ball[R:2 * R]

    A = _intra_chunk_scores(qs, k, b, (), None, seq)
    o_intra = _dot(A.astype(BF16), vb)
    qo = qs * jnp.exp2(b)
    ku = k * jnp.exp2(bl - b)
    if pad:
        ku = jnp.concatenate([ku, jnp.zeros((pad, DK), F32)], axis=0)
        vb = jnp.concatenate([vb, jnp.zeros((pad, DV), BF16)], axis=0)
    ku_t = ku.T
    lane_seq = lax.broadcasted_iota(jnp.int32, ku_t.shape, 1) // seq
    for i in range(nb):
        rows = slice(i * seq, (i + 1) * seq)
        S = sin_ref[i]
        o = _dot(qo[rows].astype(BF16), S.astype(BF16)) + o_intra[rows]
        o_ref[rows, :] = _head_norm(o)
        d_s = _dot(jnp.where(lane_seq == i, ku_t, 0.0).astype(BF16), vb)
        a_t = jnp.exp2(jnp.broadcast_to(bl[i * seq:i * seq + 1, :], (V7X_LANES, DK))).T
        for n in range(DV // V7X_LANES):
            sl = slice(n * V7X_LANES, (n + 1) * V7X_LANES)
            sout_ref[i, :, sl] = a_t * S[:, sl] + d_s[:, sl]


def _gla_kernel(*refs, n_prompt, n_sample, nt, n_carried, prompt_kw, sample_kw):
    refs = list(refs)
    p_in = [refs.pop(0) for _ in range(6)]
    s_in = [refs.pop(0) for _ in range(6)]
    del refs[:n_carried]
    p_out = [refs.pop(0) for _ in range(2)]
    s_out = [refs.pop(0) for _ in range(2)]
    (s_ref,) = refs
    s = pl.program_id(0)

    def guarded(n_steps, fn):
        if n_steps == max(n_prompt, n_sample):
            fn()
        else:
            pl.when(s < n_steps)(fn)

    guarded(n_prompt, lambda: _gla_prompt_step(s % nt, nt, *p_in, *p_out, s_ref, **prompt_kw))
    guarded(n_sample, lambda: _gla_sample_step(*s_in, *s_out, **sample_kw))


def _gla(h, la, state, new_state, layer, batch, seq, dbatch, dseq, heads, dk, dv, q_off, k_off, v_off):
    C = GLA_CHUNK_ROWS
    nt = seq // C
    groups = []
    g = C
    while g > GLA_SUB:
        groups.append(g)
        g //= 2
    cmat_p = _cumsum_mat(C)
    lmask = _level_masks(C, groups)
    wk, wv = heads * dk, heads * dv
    assert q_off % wk == 0 and k_off % wk == 0 and v_off % wv == 0
    pq, pk, pv = q_off // wk, k_off // wk, v_off // wv
    n_prompt = batch * nt

    R = GLA_SAMPLE_ROWS
    nb = R // dseq
    rb = batch * seq // R
    cmat_s = _seq_cumsum_mats(R, dseq)
    sq, sk, sv = q_off // dk, k_off // dk, v_off // dv
    n_sample = (dbatch // nb) * heads
    n_steps = max(n_prompt, n_sample)

    def pmap(fn):
        return lambda s: fn(jnp.minimum(s, n_prompt - 1))

    def smap(fn):
        return lambda s: fn(jnp.minimum(s, n_sample - 1) // heads, jnp.minimum(s, n_sample - 1) % heads)

    carried = [] if new_state is None else [new_state]
    return pl.pallas_call(
        functools.partial(
            _gla_kernel, n_prompt=n_prompt, n_sample=n_sample, nt=nt, n_carried=len(carried),
            prompt_kw=dict(scale=float(dk) ** -0.5, sub=GLA_SUB, groups=tuple(groups)),
            sample_kw=dict(scale=float(dk) ** -0.5, seq=dseq)),
        grid=(n_steps,),
        in_specs=[
            pl.BlockSpec((C, wk), pmap(lambda c: (c, pq))),
            pl.BlockSpec((C, wk), pmap(lambda c: (c, pk))),
            pl.BlockSpec((C, wv), pmap(lambda c: (c, pv))),
            pl.BlockSpec((C, wk), pmap(lambda c: (c, 0))),
            pl.BlockSpec(cmat_p.shape, lambda s: (0, 0)),
            pl.BlockSpec(lmask.shape, lambda s: (0, 0)),
            pl.BlockSpec((R, dk), smap(lambda r, hi: (rb + r, sq + hi))),
            pl.BlockSpec((R, dk), smap(lambda r, hi: (rb + r, sk + hi))),
            pl.BlockSpec((R, dv), smap(lambda r, hi: (rb + r, sv + hi))),
            pl.BlockSpec((R, dk), smap(lambda r, hi: (rb + r, hi))),
            pl.BlockSpec(cmat_s.shape, lambda s: (0, 0)),
            pl.BlockSpec((None, nb, None, dk, dv), smap(lambda r, hi: (layer, r, hi, 0, 0))),
        ] + [pl.BlockSpec(memory_space=pl.ANY)] * len(carried),
        out_specs=[
            pl.BlockSpec((C, wv), pmap(lambda c: (c, 0))),
            pl.BlockSpec((None, heads, dk, dv), pmap(lambda c: (c // nt, 0, 0, 0))),
            pl.BlockSpec((R, dv), smap(lambda r, hi: (r, hi))),
            pl.BlockSpec((None, nb, None, dk, dv), smap(lambda r, hi: (layer, r, hi, 0, 0))),
        ],
        out_shape=[
            jax.ShapeDtypeStruct((batch * seq, heads * dv), F32),
            jax.ShapeDtypeStruct((batch, heads, dk, dv), F32),
            jax.ShapeDtypeStruct((dbatch * dseq, heads * dv), F32),
            jax.ShapeDtypeStruct(state.shape, F32),
        ],
        input_output_aliases={12: 3} if carried else {},
        scratch_shapes=[pltpu.VMEM((heads, dk, dv), F32)],
        compiler_params=_cparams(("arbitrary",)),
        name="gla",
    )(h, h, h, la, cmat_p, lmask, h, h, h, la, cmat_s, state, *carried)


def _window_means_minus_input(ext, u, pos, lead):
    n, pw = u.shape
    gw = pw // len(POOL_WINDOWS)
    outs = []
    for g, w in enumerate(POOL_WINDOWS):
        cols = slice(g * gw, (g + 1) * gw)
        p = ext[:, cols]
        span = 1
        while span < w:
            p = p + pltpu.roll(p, span, axis=0)
            span *= 2
        cnt = jnp.minimum(w, pos + 1).astype(F32)
        outs.append(p[lead:lead + n] / cnt - u[:, cols])
    return outs


def _mix_tail(pooled, on, r, ga, gb, x, hg_ref, wgo_ref, pw_ref, ps_ref, wo_ref, g_ref, b_ref, alpha):
    ya = jnp.concatenate(
        [_dot(p.astype(BF16), pw_ref[g]) for g, p in enumerate(pooled)], axis=1) * ps_ref[...]
    yb = _dot((on * hg_ref[...] * (r * jax.nn.sigmoid(r))).astype(BF16), wgo_ref[...])
    merged = jax.nn.sigmoid(ga) * ya + jax.nn.sigmoid(gb) * yb
    m = _dot(merged.astype(BF16), wo_ref[...])
    return _layer_norm(alpha * x + m, g_ref[...], b_ref[...])


def _post_prompt_kernel(on_ref, r_ref, ga_ref, gb_ref, u_ref, x_ref, hg_ref, wgo_ref, pw_ref, ps_ref,
                        wo_ref, g_ref, b_ref, o_ref, npool_ref, halo_ref, *, tiles_per_seq, alpha):
    i = pl.program_id(0)
    tm = u_ref.shape[0]
    lead = halo_ref.shape[0]
    ti = i % tiles_per_seq

    @pl.when(ti == 0)
    def _():
        halo_ref[...] = jnp.zeros_like(halo_ref)

    u = u_ref[...]
    ext = jnp.concatenate([halo_ref[...], u], axis=0)
    halo_ref[...] = u[tm - lead:, :]

    @pl.when(ti == tiles_per_seq - 1)
    def _():
        npool_ref[...] = u[tm - POOL_BUF:, :]

    pos = ti * tm + lax.broadcasted_iota(jnp.int32, (tm, 1), 0)
    pooled = _window_means_minus_input(ext, u, pos, lead)
    o_ref[...] = _mix_tail(pooled, on_ref[...], r_ref[...], ga_ref[...], gb_ref[...], x_ref[...],
                           hg_ref, wgo_ref, pw_ref, ps_ref, wo_ref, g_ref, b_ref, alpha)


def _post_sample_kernel(on_ref, r_ref, ga_ref, gb_ref, u_ref, x_ref, st_ref, hg_ref, wgo_ref, pw_ref,
                        ps_ref, wo_ref, g_ref, b_ref, carried_ref, o_ref, npool_ref, ext_ref,
                        *, seq, start, alpha):
    del carried_ref
    tm, pw = u_ref.shape
    nb = tm // seq
    lead = ext_ref.shape[1] - seq
    u = u_ref[...]
    u3 = u.reshape(nb, seq, pw)
    st = st_ref[...]
    ext_ref[:, 0:lead - POOL_BUF, :] = jnp.zeros((nb, lead - POOL_BUF, pw), F32)
    ext_ref[:, lead - POOL_BUF:lead, :] = st
    ext_ref[:, lead:, :] = u3
    e3 = ext_ref[...]
    npool_ref[...] = e3[:, lead + seq - POOL_BUF:, :]
    p = e3.reshape(nb * (lead + seq), pw)
    gw = pw // len(POOL_WINDOWS)
    pos = start + lax.broadcasted_iota(jnp.int32, (nb, seq, 1), 1)
    pooled = []
    for g, w in enumerate(POOL_WINDOWS):
        cols = slice(g * gw, (g + 1) * gw)
        pg = p[:, cols]
        span = 1
        while span < w:
            pg = pg + pltpu.roll(pg, span, axis=0)
            span *= 2
        win = pg.reshape(nb, lead + seq, gw)[:, lead:, :]
        cnt = jnp.minimum(w, pos + 1).astype(F32)
        pooled.append((win / cnt - u3[:, :, cols]).reshape(tm, gw))
    o_ref[...] = _mix_tail(pooled, on_ref[...], r_ref[...], ga_ref[...], gb_ref[...], x_ref[...],
                           hg_ref, wgo_ref, pw_ref, ps_ref, wo_ref, g_ref, b_ref, alpha)


def _const_spec(shape, index_map):
    return pl.BlockSpec(shape, index_map, pipeline_mode=pl.Buffered(1))


def _post_weight_specs(layer, ln_idx, D, GV, groups, gw, ogw, n):
    z = (0,) * (n - 1)
    return [
        _const_spec((None, 1, GV), lambda *a: (layer, 0, 0)),
        _const_spec((None, GV, D), lambda *a: (layer, 0, 0)),
        _const_spec((None, groups, gw, ogw), lambda *a: (layer, 0, 0, 0)),
        _const_spec((None, 1, D), lambda *a: (layer, 0, 0)),
        _const_spec((None, D, D), lambda *a: (layer, 0, 0)),
        _const_spec((None, None, 1, D), lambda *a: (layer, ln_idx, 0, 0)),
        _const_spec((None, None, 1, D), lambda *a: (layer, ln_idx, 0, 0)),
    ]


def _post_prompt(on, h, x, weights, layer, ln_idx, alpha, seq, tm, cols):
    TP, GV = on.shape
    T, D = x.shape
    hg, wgo, pw, ps, wo, ln_g, ln_b = weights
    groups, gw, ogw = pw.shape[1:]
    PW = groups * gw
    tiles_per_seq = seq // tm
    r_b, ga_b, gb_b, u_b = cols["r"] // GV, cols["ga"] // D, cols["gb"] // D, cols["u"] // PW
    lead = 2 * V7X_SUBLANES
    return pl.pallas_call(
        functools.partial(_post_prompt_kernel, tiles_per_seq=tiles_per_seq, alpha=alpha),
        grid=(TP // tm,),
        in_specs=[
            pl.BlockSpec((tm, GV), lambda i: (i, 0)),
            pl.BlockSpec((tm, GV), lambda i: (i, r_b)),
            pl.BlockSpec((tm, D), lambda i: (i, ga_b)),
            pl.BlockSpec((tm, D), lambda i: (i, gb_b)),
            pl.BlockSpec((tm, PW), lambda i: (i, u_b)),
            pl.BlockSpec((tm, D), lambda i: (i, 0)),
        ] + _post_weight_specs(layer, ln_idx, D, GV, groups, gw, ogw, 1),
        out_specs=[
            pl.BlockSpec((tm, D), lambda i: (i, 0)),
            pl.BlockSpec((None, POOL_BUF, PW), lambda i: (i // tiles_per_seq, 0, 0)),
        ],
        out_shape=[
            jax.ShapeDtypeStruct((T, D), F32),
            jax.ShapeDtypeStruct((TP // seq, POOL_BUF, PW), F32),
        ],
        scratch_shapes=[pltpu.VMEM((lead, PW), F32)],
        compiler_params=_cparams(("arbitrary",)),
        name="post_prompt",
    )(on, h, h, h, h, x, hg, wgo, pw, ps, wo, ln_g, ln_b)


def _post_sample(on, h, x, x2, state_pool, weights, layer, ln_idx, alpha, row0, seq, tm, cols):
    TS, GV = on.shape
    D = x.shape[1]
    hg, wgo, pw, ps, wo, ln_g, ln_b = weights
    groups, gw, ogw = pw.shape[1:]
    PW = groups * gw
    nb = tm // seq
    batch = TS // seq
    rb = row0 // tm
    r_b, ga_b, gb_b, u_b = cols["r"] // GV, cols["ga"] // D, cols["gb"] // D, cols["u"] // PW
    lead = 2 * V7X_SUBLANES
    return pl.pallas_call(
        functools.partial(_post_sample_kernel, seq=seq, start=PAST_LEN, alpha=alpha),
        grid=(TS // tm,),
        in_specs=[
            pl.BlockSpec((tm, GV), lambda i: (i, 0)),
            pl.BlockSpec((tm, GV), lambda i: (rb + i, r_b)),
            pl.BlockSpec((tm, D), lambda i: (rb + i, ga_b)),
            pl.BlockSpec((tm, D), lambda i: (rb + i, gb_b)),
            pl.BlockSpec((tm, PW), lambda i: (rb + i, u_b)),
            pl.BlockSpec((tm, D), lambda i: (rb + i, 0)),
            pl.BlockSpec((None, nb, POOL_BUF, PW), lambda i: (layer, i, 0, 0)),
        ] + _post_weight_specs(layer, ln_idx, D, GV, groups, gw, ogw, 1) + [
            pl.BlockSpec(memory_space=pl.ANY),
        ],
        out_specs=[
            pl.BlockSpec((tm, D), lambda i: (rb + i, 0)),
            pl.BlockSpec((nb, POOL_BUF, PW), lambda i: (i, 0, 0)),
        ],
        out_shape=[
            jax.ShapeDtypeStruct(x2.shape, F32),
            jax.ShapeDtypeStruct((batch, POOL_BUF, PW), F32),
        ],
        input_output_aliases={14: 0},
        scratch_shapes=[pltpu.VMEM((nb, lead + seq, PW), F32)],
        compiler_params=_cparams(("arbitrary",)),
        name="post_sample",
    )(on, h, h, h, h, x, state_pool, hg, wgo, pw, ps, wo, ln_g, ln_b, x2)


def _tiles(T, TP, seq_p, F, NH):
    def fit(n, t):
        while n % t:
            t //= 2
        return t
    return dict(
        ffn_tm=fit(np.gcd(T, TP), 1024), ffn_tf=fit(F, 512),
        proj_tm=fit(np.gcd(T, TP), 1024), proj_tn=fit(NH, 1024),
        post_tm=fit(seq_p, 256), post_sample_tm=fit(np.gcd(T - TP, TP), 128),
    )


def kernel(x_prompt, x_sample, state_pool, state_gla, ln_g, ln_b, w_ffn_in, w_ffn_out, w_in, pool_w,
           pool_scale, a_up, a_bias, head_g, w_gla_out, w_out):
    B, SEQ, D = x_prompt.shape
    DB, DSEQ, _ = x_sample.shape
    L = w_in.shape[0]
    PW = state_pool.shape[-1]
    H, DK, DV = state_gla.shape[2:]
    GK, GV = H * DK, H * DV
    RANK = a_up.shape[1]
    F = w_ffn_out.shape[2]
    TP, TS = B * SEQ, DB * DSEQ
    T = TP + TS
    alpha = float((2 * L) ** 0.25)
    assert SEQ % GLA_CHUNK_ROWS == 0 and TP % V7X_LANES == 0 and V7X_LANES % DSEQ == 0
    assert DSEQ == V7X_SUBLANES and PAST_LEN >= POOL_BUF

    seg = (PW + 2 * GK, GV, RANK, GV + 2 * D)
    assert sum(seg) == w_in.shape[2]
    w_in_b = w_in.astype(BF16)
    aup = jnp.pad(a_up, ((0, 0), (0, V7X_LANES - RANK), (0, 0))).astype(BF16)
    abias = a_bias.reshape(L, 1, GK)
    ln_g4 = ln_g.reshape(L, 3, 1, D)
    ln_b4 = ln_b.reshape(L, 3, 1, D)
    post_w = (head_g.reshape(L, 1, GV), w_gla_out.astype(BF16), pool_w.astype(BF16),
              pool_scale.reshape(L, 1, D), w_out.astype(BF16), ln_g4, ln_b4)
    c_v = GV + 2 * D
    c_u = c_v + GV
    cols = dict(r=0, ga=GV, gb=GV + D, v=c_v, u=c_u, q=c_u + PW, k=c_u + PW + GK)
    ts = _tiles(T, TP, SEQ, F, w_in.shape[2] - RANK)
    assert GV == D and c_v % DV == 0 and c_u % PW == 0 and cols["q"] % DK == 0

    ffn = functools.partial(_ffn_ln, w_ffn_in=w_ffn_in, w_ffn_out=w_ffn_out, ln_g=ln_g4, ln_b=ln_b4,
                            alpha=alpha, tm=ts["ffn_tm"], tf=ts["ffn_tf"])

    xs = [x_prompt.reshape(TP, D), x_sample.reshape(TS, D)]
    pool_p, gla_p, pool_s, gla_s = [], [], [], None
    for l in range(L):
        (x,) = ffn(xs, [T], layer=l, which=0, ln_idx=0)
        h, la = _proj(x, w_in_b, aup, abias, l, ts["proj_tm"], ts["proj_tn"], seg)
        on_p, s_p, on_s, gla_s = _gla(h, la, state_gla, gla_s, l, B, SEQ, DB, DSEQ, H, DK, DV,
                                      cols["q"], cols["k"], cols["v"])
        x2, np_p = _post_prompt(on_p, h, x, post_w, l, 1, alpha, SEQ, ts["post_tm"], cols)
        x2, np_s = _post_sample(on_s, h, x, x2, state_pool, post_w, l, 1, alpha, TP, DSEQ,
                                ts["post_sample_tm"], cols)
        xs = ffn([x2], [T] if l + 1 < L else [TP, TS], layer=l, which=1, ln_idx=2)
        pool_p.append(np_p)
        gla_p.append(s_p)
        pool_s.append(np_s)
    return (xs[0].reshape(B, SEQ, D), xs[1].reshape(DB, DSEQ, D), jnp.stack(pool_p),
            jnp.stack(gla_p), jnp.stack(pool_s), gla_s)
```

```python
import functools

import numpy as np
import jax
import jax.numpy as jnp
from jax import lax
from jax.experimental import pallas as pl
from jax.experimental.pallas import tpu as pltpu

POOL_WINDOWS = (2, 4, 8, 16)
POOL_BUF = max(POOL_WINDOWS) - 1
GLA_TAU = 16.0
LN_EPS = 1e-5
HEAD_NORM_EPS = 1e-6
PAST_LEN = 16384

V7X_LANES = 128
V7X_SUBLANES = 8
V7X_VMEM_BYTES = 64 * 1024 * 1024
V7X_VMEM_LIMIT_BYTES = V7X_VMEM_BYTES - 1024 * 1024

F32 = jnp.float32
BF16 = jnp.bfloat16

GLA_SUB = 4
GLA_CHUNK_ROWS = 128
GLA_SAMPLE_ROWS = 64
POST_ROW_SPLITS = 2
LOG2_E = 1.4426950408889634


def _cparams(sem):
    return pltpu.CompilerParams(dimension_semantics=sem, vmem_limit_bytes=V7X_VMEM_LIMIT_BYTES)


def _layer_norm(y, g, b):
    mu = jnp.mean(y, axis=-1, keepdims=True)
    yc = y - mu
    var = jnp.mean(yc * yc, axis=-1, keepdims=True)
    return yc * lax.rsqrt(var + LN_EPS) * g + b


def _dot(a, b):
    return jnp.dot(a, b, preferred_element_type=F32)


def _dot_nt(a, b):
    return lax.dot_general(a, b, (((1,), (1,)), ((), ())), preferred_element_type=F32)


def _split3(x):
    p0 = x.astype(BF16)
    r1 = x - p0.astype(F32)
    p1 = r1.astype(BF16)
    p2 = (r1 - p1.astype(F32)).astype(BF16)
    return p0, p1, p2


FFN_DOWN_CHUNKS = 4


def _ffn_ln_kernel(*refs, alpha, n_carried, cast_weights):
    refs = list(refs)
    x_ref, wg_ref, wu_ref, wo_ref, g_ref, b_ref = [refs.pop(0) for _ in range(6)]
    del refs[:n_carried]
    o_ref = refs.pop(0)
    wb_refs = [refs.pop(0) for _ in range(3)] if cast_weights else None
    (xb_ref,) = refs
    j = pl.program_id(1)

    def ff_step(first):
        wg, wu, wo = wg_ref[...], wu_ref[...], wo_ref[...]
        if cast_weights:
            wg, wu, wo = wg.astype(BF16), wu.astype(BF16), wo.astype(BF16)
            for w_ref, w in zip(wb_refs, (wg, wu, wo)):
                w_ref[...] = w
        if first:
            xb = x_ref[...].astype(BF16)
            xb_ref[...] = xb
        else:
            xb = xb_ref[...]
        gate = _dot(xb, wg)
        up = _dot(xb, wu)
        act = (gate * jax.nn.sigmoid(gate) * up).astype(BF16)
        cw = o_ref.shape[1] // FFN_DOWN_CHUNKS
        for c in range(FFN_DOWN_CHUNKS):
            cs = slice(c * cw, (c + 1) * cw)
            if first:
                o_ref[:, cs] = _dot(act, wo[:, cs])
            else:
                o_ref[:, cs] += _dot(act, wo[:, cs])

    pl.when(j == 0)(lambda: ff_step(True))
    pl.when(j > 0)(lambda: ff_step(False))

    @pl.when(j == pl.num_programs(1) - 1)
    def _():
        y = alpha * x_ref[...] + 0.5 * o_ref[...]
        o_ref[...] = _layer_norm(y, g_ref[...], b_ref[...])


def _ffn_ln(xs, out_rows, w_ffn_in, w_ffn_out, ln_g, ln_b, layer, which, ln_idx, alpha, tm, tf):
    D = xs[0].shape[1]
    F = w_ffn_out.shape[2]
    ln_specs = [pl.BlockSpec((None, None, 1, D), lambda i, j: (layer, ln_idx, 0, 0))] * 2
    xb_scratch = pltpu.VMEM((tm, D), BF16)

    def tile_owner(rows_list):
        return [(a, t) for a, rows in enumerate(rows_list) for t in range(rows // tm)]
    runs = []
    for (sa, st), (da, dt) in zip(tile_owner([x.shape[0] for x in xs]), tile_owner(out_rows)):
        if runs and runs[-1][0] == sa and runs[-1][2] == da:
            runs[-1][4] += 1
        else:
            runs.append([sa, st, da, dt, 1])

    tf0 = tf // 2
    nf0 = F // tf0
    sa, st, da, dt, _ = runs[0]
    y0, wg_b, wu_b, wo_b = pl.pallas_call(
        functools.partial(_ffn_ln_kernel, alpha=alpha, n_carried=0, cast_weights=True),
        grid=(1, nf0),
        in_specs=[
            pl.BlockSpec((tm, D), lambda i, j: (st, 0), pipeline_mode=pl.Buffered(1)),
            pl.BlockSpec((None, None, D, tf0), lambda i, j: (layer, which, 0, j)),
            pl.BlockSpec((None, None, D, tf0), lambda i, j: (layer, which, 0, j + nf0)),
            pl.BlockSpec((None, None, tf0, D), lambda i, j: (layer, which, j, 0)),
        ] + ln_specs,
        out_specs=[
            pl.BlockSpec((tm, D), lambda i, j: (dt, 0)),
            pl.BlockSpec((D, tf0), lambda i, j: (0, j)),
            pl.BlockSpec((D, tf0), lambda i, j: (0, j)),
            pl.BlockSpec((tf0, D), lambda i, j: (j, 0)),
        ],
        out_shape=[
            jax.ShapeDtypeStruct((out_rows[da], D), F32),
            jax.ShapeDtypeStruct((D, F), BF16),
            jax.ShapeDtypeStruct((D, F), BF16),
            jax.ShapeDtypeStruct((F, D), BF16),
        ],
        scratch_shapes=[xb_scratch],
        compiler_params=_cparams(("arbitrary", "arbitrary")),
        name="ffn_ln_first",
    )(xs[sa], w_ffn_in, w_ffn_in, w_ffn_out, ln_g, ln_b)
    outs = {da: y0}
    runs[0][1] += 1
    runs[0][3] += 1
    runs[0][4] -= 1

    nf = F // tf
    for sa, st, da, dt, n in runs:
        if n == 0:
            continue
        carried = [outs[da]] if da in outs else []
        outs[da] = pl.pallas_call(
            functools.partial(_ffn_ln_kernel, alpha=alpha, n_carried=len(carried), cast_weights=False),
            grid=(n, nf),
            in_specs=[
                pl.BlockSpec((tm, D), lambda i, j, st=st: (i + st, 0)),
                pl.BlockSpec((D, tf), lambda i, j: (0, j)),
                pl.BlockSpec((D, tf), lambda i, j: (0, j)),
                pl.BlockSpec((tf, D), lambda i, j: (j, 0)),
            ] + ln_specs + [pl.BlockSpec(memory_space=pl.ANY)] * len(carried),
            out_specs=pl.BlockSpec((tm, D), lambda i, j, dt=dt: (i + dt, 0)),
            out_shape=jax.ShapeDtypeStruct((out_rows[da], D), F32),
            input_output_aliases={6: 0} if carried else {},
            scratch_shapes=[xb_scratch],
            compiler_params=_cparams(("arbitrary", "arbitrary")),
            name="ffn_ln",
        )(xs[sa], wg_b, wu_b, wo_b, ln_g, ln_b, *carried)
    return [outs[a] for a in range(len(out_rows))]


def _proj_row_prologue(x_ref, xb_ref, wa, aup_ref, ab_ref, la_ref):
    xb = x_ref[...].astype(BF16)
    xb_ref[...] = xb
    a_lo = _dot(xb, wa)
    z = _dot(a_lo.astype(BF16), aup_ref[...]) + ab_ref[...]
    la_ref[...] = (jnp.minimum(z, 0.0) - jnp.log(1.0 + jnp.exp(-jnp.abs(z)))) / GLA_TAU


def _proj_first_kernel(x_ref, wsrc_ref, wnext_ref, walo_ref, aup_ref, ab_ref, h_ref, la_ref, wm_ref,
                       wa_ref, xb_ref, *, n_shift, rank):
    j = pl.program_id(1)
    tn = wsrc_ref.shape[1]

    @pl.when(j == 0)
    def _():
        lane = lax.broadcasted_iota(jnp.int32, walo_ref.shape, 1)
        wa = jnp.where(lane < rank, walo_ref[...].astype(F32), 0.0).astype(BF16)
        wa_ref[...] = wa
        _proj_row_prologue(x_ref, xb_ref, wa, aup_ref, ab_ref, la_ref)

    def emit(w):
        wm_ref[...] = w
        h_ref[...] = _dot(xb_ref[...], w)

    @pl.when(j < n_shift)
    def _():
        both = jnp.concatenate([wsrc_ref[...], wnext_ref[...]], axis=1).astype(F32)
        emit(pltpu.roll(both, both.shape[1] - rank, axis=1)[:, :tn].astype(BF16))

    @pl.when(j >= n_shift)
    def _():
        emit(wsrc_ref[...])


def _proj_kernel(x_ref, wm_ref, wa_ref, aup_ref, ab_ref, h_in_ref, la_in_ref, h_ref, la_ref, xb_ref):
    del h_in_ref, la_in_ref
    j = pl.program_id(1)

    @pl.when(j == 0)
    def _():
        _proj_row_prologue(x_ref, xb_ref, wa_ref[...], aup_ref, ab_ref, la_ref)
        h_ref[...] = _dot(x_ref[...].astype(BF16), wm_ref[...])

    @pl.when(j > 0)
    def _():
        h_ref[...] = _dot(xb_ref[...], wm_ref[...])


def _proj(x, w_in, aup, abias, layer, tm, tn, seg):
    T, D = x.shape
    GK = aup.shape[2]
    head, mid, rank, tail = seg
    lo0 = head + mid
    NH = lo0 + tail
    tn0 = tn // 2
    assert lo0 % tn0 == 0 and head % tn0 == 0 and tail % tn0 == 0 and lo0 % V7X_LANES == 0
    n_shift, n_al = tail // tn0, lo0 // tn0
    per_tile = tn0 // V7X_LANES

    def src_block(j):
        return jnp.where(j < n_shift, n_al + j, (j - n_shift + head // tn0) % n_al)

    def next_block(j):
        return (n_al + jnp.minimum(j, n_shift - 1) + 1) * per_tile

    const = lambda i, j: (layer, 0, 0)
    h0, la0, wm, wa = pl.pallas_call(
        functools.partial(_proj_first_kernel, n_shift=n_shift, rank=rank),
        grid=(1, NH // tn0),
        in_specs=[
            pl.BlockSpec((tm, D), lambda i, j: (0, 0), pipeline_mode=pl.Buffered(1)),
            pl.BlockSpec((None, D, tn0), lambda i, j: (layer, 0, src_block(j))),
            pl.BlockSpec((None, D, V7X_LANES), lambda i, j: (layer, 0, next_block(j))),
            pl.BlockSpec((None, D, V7X_LANES), lambda i, j: (layer, 0, lo0 // V7X_LANES)),
            pl.BlockSpec((None, V7X_LANES, GK), const),
            pl.BlockSpec((None, 1, GK), const),
        ],
        out_specs=[
            pl.BlockSpec((tm, tn0), lambda i, j: (0, j)),
            pl.BlockSpec((tm, GK), lambda i, j: (0, 0)),
            pl.BlockSpec((D, tn0), lambda i, j: (0, j)),
            pl.BlockSpec((D, V7X_LANES), lambda i, j: (0, 0)),
        ],
        out_shape=[
            jax.ShapeDtypeStruct((T, NH), F32), jax.ShapeDtypeStruct((T, GK), F32),
            jax.ShapeDtypeStruct((D, NH), BF16), jax.ShapeDtypeStruct((D, V7X_LANES), BF16),
        ],
        scratch_shapes=[pltpu.VMEM((tm, D), BF16)],
        compiler_params=_cparams(("arbitrary", "arbitrary")),
        name="proj_first",
    )(x, w_in, w_in, w_in, aup, abias)

    return pl.pallas_call(
        _proj_kernel,
        grid=(T // tm - 1, NH // tn),
        in_specs=[
            pl.BlockSpec((tm, D), lambda i, j: (i + 1, 0)),
            pl.BlockSpec((D, tn), lambda i, j: (0, j)),
            pl.BlockSpec((D, V7X_LANES), lambda i, j: (0, 0)),
            pl.BlockSpec((None, V7X_LANES, GK), const),
            pl.BlockSpec((None, 1, GK), const),
            pl.BlockSpec(memory_space=pl.ANY),
            pl.BlockSpec(memory_space=pl.ANY),
        ],
        out_specs=[
            pl.BlockSpec((tm, tn), lambda i, j: (i + 1, j)),
            pl.BlockSpec((tm, GK), lambda i, j: (i + 1, 0)),
        ],
        out_shape=[jax.ShapeDtypeStruct((T, NH), F32), jax.ShapeDtypeStruct((T, GK), F32)],
        input_output_aliases={5: 0, 6: 1},
        scratch_shapes=[pltpu.VMEM((tm, D), BF16)],
        compiler_params=_cparams(("parallel", "arbitrary")),
        name="proj",
    )(x, wm, wa, aup, abias, h0, la0)


def _stack01(mats, dtype):
    return jnp.asarray(np.concatenate(mats, axis=0).astype(np.float32), dtype=dtype)


def _cumsum_mat(rows):
    i = np.arange(rows)[:, None]
    j = np.arange(rows)[None, :]
    return _stack01([j <= i], BF16)


def _seq_cumsum_mats(rows, seq):
    i = np.arange(rows)[:, None]
    j = np.arange(rows)[None, :]
    same = i // seq == j // seq
    return _stack01([same & (j <= i), same], BF16)


def _level_masks(rows, groups):
    i = np.arange(rows)[:, None]
    j = np.arange(rows)[None, :]
    out = []
    for grp in groups:
        half = grp // 2
        out.append((i // grp == j // grp) & (i % grp >= half) & (j % grp < half))
    return _stack01(out, F32)


def _intra_chunk_scores(qs, k, b, groups, lmask_ref, sub):
    C = qs.shape[0]
    A = jnp.zeros((C, C), F32)
    for n, grp in enumerate(groups):
        e = jnp.concatenate(
            [jnp.exp2(-jnp.abs(b[s:s + grp] - b[s + grp // 2 - 1:s + grp // 2])) for s in range(0, C, grp)],
            axis=0)
        A = A + _dot_nt((qs * e).astype(BF16), (k * e).astype(BF16)) * lmask_ref[n * C:(n + 1) * C, :]
    row_in_sub = lax.broadcasted_iota(jnp.int32, (C, 1), 0) & (sub - 1)
    delta = lax.broadcasted_iota(jnp.int32, (C, C), 0) - lax.broadcasted_iota(jnp.int32, (C, C), 1)
    for d in range(sub):
        if d == 0:
            t = qs * k
        else:
            kd = pltpu.roll(k, d, axis=0)
            bd = pltpu.roll(b, d, axis=0)
            t = qs * kd * jnp.exp2(jnp.minimum(b - bd, 0.0))
        red = jnp.sum(t, axis=1, keepdims=True)
        red = jnp.where(row_in_sub >= d, red, 0.0)
        A = A + jnp.where(delta == d, red, 0.0)
    return A


def _head_norm(o):
    return o * lax.rsqrt(jnp.mean(o * o, axis=-1, keepdims=True) + HEAD_NORM_EPS)


def _gla_prompt_step(t, nt, q_ref, k_ref, v_ref, la_ref, cmat_ref, lmask_ref, o_ref, sout_ref, s_ref,
                     *, scale, sub, groups):
    C = q_ref.shape[0]
    HB, DK, DV = s_ref.shape

    @pl.when(t == 0)
    def _():
        s_ref[...] = jnp.zeros_like(s_ref)

    cm = cmat_ref[...]
    p0, p1, p2 = _split3(la_ref[...] * LOG2_E)
    b_all = _dot(cm, p0) + _dot(cm, p1) + _dot(cm, p2)
    for hh in range(HB):
        kc = slice(hh * DK, (hh + 1) * DK)
        vc = slice(hh * DV, (hh + 1) * DV)
        qs = q_ref[:, kc] * scale
        k = k_ref[:, kc]
        vb = v_ref[:, vc].astype(BF16)
        b = b_all[:, kc]

        A = _intra_chunk_scores(qs, k, b, groups, lmask_ref, sub)
        S = s_ref[hh]
        o = _dot((qs * jnp.exp2(b)).astype(BF16), S.astype(BF16)) + _dot(A.astype(BF16), vb)
        o_ref[:, vc] = _head_norm(o)

        bl = b[C - 1:C, :]
        ku_t = (k * jnp.exp2(bl - b)).T
        d_s = _dot(ku_t.astype(BF16), vb)
        a_t = jnp.exp2(jnp.broadcast_to(bl, (V7X_LANES, DK))).T
        for n in range(DV // V7X_LANES):
            sl = slice(n * V7X_LANES, (n + 1) * V7X_LANES)
            s_ref[hh, :, sl] = a_t * S[:, sl] + d_s[:, sl]

    @pl.when(t == nt - 1)
    def _():
        sout_ref[...] = s_ref[...]


def _gla_sample_step(q_ref, k_ref, v_ref, la_ref, cmat_ref, sin_ref, o_ref, sout_ref, *, scale, seq):
    R, DK = q_ref.shape
    DV = v_ref.shape[1]
    nb = R // seq
    pad = V7X_LANES - R

    qs = q_ref[...] * scale
    k = k_ref[...]
    vb = v_ref[...].astype(BF16)
    cm = cmat_ref[...]
    p0, p1, p2 = _split3(la_ref[...] * LOG2_E)
    ball = _dot(cm, p0) + _dot(cm, p1) + _dot(cm, p2)
    b = ball[0:R]
    bl = ball[R:2 * R]

    A = _intra_chunk_scores(qs, k, b, (), None, seq)
    o_intra = _dot(A.astype(BF16), vb)
    qo = qs * jnp.exp2(b)
    ku = k * jnp.exp2(bl - b)
    if pad:
        ku = jnp.concatenate([ku, jnp.zeros((pad, DK), F32)], axis=0)
        vb = jnp.concatenate([vb, jnp.zeros((pad, DV), BF16)], axis=0)
    ku_t = ku.T
    lane_seq = lax.broadcasted_iota(jnp.int32, ku_t.shape, 1) // seq
    for i in range(nb):
        rows = slice(i * seq, (i + 1) * seq)
        S = sin_ref[i]
        o = _dot(qo[rows].astype(BF16), S.astype(BF16)) + o_intra[rows]
        o_ref[rows, :] = _head_norm(o)
        d_s = _dot(jnp.where(lane_seq == i, ku_t, 0.0).astype(BF16), vb)
        a_t = jnp.exp2(jnp.broadcast_to(bl[i * seq:i * seq + 1, :], (V7X_LANES, DK))).T
        for n in range(DV // V7X_LANES):
            sl = slice(n * V7X_LANES, (n + 1) * V7X_LANES)
            sout_ref[i, :, sl] = a_t * S[:, sl] + d_s[:, sl]


def _gla_kernel(*refs, n_prompt, n_sample, nt, n_carried, prompt_kw, sample_kw):
    refs = list(refs)
    p_in = [refs.pop(0) for _ in range(6)]
    s_in = [refs.pop(0) for _ in range(6)]
    del refs[:n_carried]
    p_out = [refs.pop(0) for _ in range(2)]
    s_out = [refs.pop(0) for _ in range(2)]
    (s_ref,) = refs
    s = pl.program_id(0)

    def guarded(n_steps, fn):
        if n_steps == max(n_prompt, n_sample):
            fn()
        else:
            pl.when(s < n_steps)(fn)

    guarded(n_prompt, lambda: _gla_prompt_step(s % nt, nt, *p_in, *p_out, s_ref, **prompt_kw))
    guarded(n_sample, lambda: _gla_sample_step(*s_in, *s_out, **sample_kw))


def _gla(h, la, state, new_state, layer, batch, seq, dbatch, dseq, heads, dk, dv, q_off, k_off, v_off):
    C = GLA_CHUNK_ROWS
    nt = seq // C
    groups = []
    g = C
    while g > GLA_SUB:
        groups.append(g)
        g //= 2
    cmat_p = _cumsum_mat(C)
    lmask = _level_masks(C, groups)
    wk, wv = heads * dk, heads * dv
    assert q_off % wk == 0 and k_off % wk == 0 and v_off % wv == 0
    pq, pk, pv = q_off // wk, k_off // wk, v_off // wv
    n_prompt = batch * nt

    R = GLA_SAMPLE_ROWS
    nb = R // dseq
    rb = batch * seq // R
    cmat_s = _seq_cumsum_mats(R, dseq)
    sq, sk, sv = q_off // dk, k_off // dk, v_off // dv
    n_sample = (dbatch // nb) * heads
    n_steps = max(n_prompt, n_sample)

    def pmap(fn):
        return lambda s: fn(jnp.minimum(s, n_prompt - 1))

    def smap(fn):
        return lambda s: fn(jnp.minimum(s, n_sample - 1) // heads, jnp.minimum(s, n_sample - 1) % heads)

    carried = [] if new_state is None else [new_state]
    return pl.pallas_call(
        functools.partial(
            _gla_kernel, n_prompt=n_prompt, n_sample=n_sample, nt=nt, n_carried=len(carried),
            prompt_kw=dict(scale=float(dk) ** -0.5, sub=GLA_SUB, groups=tuple(groups)),
            sample_kw=dict(scale=float(dk) ** -0.5, seq=dseq)),
        grid=(n_steps,),
        in_specs=[
            pl.BlockSpec((C, wk), pmap(lambda c: (c, pq))),
            pl.BlockSpec((C, wk), pmap(lambda c: (c, pk))),
            pl.BlockSpec((C, wv), pmap(lambda c: (c, pv))),
            pl.BlockSpec((C, wk), pmap(lambda c: (c, 0))),
            pl.BlockSpec(cmat_p.shape, lambda s: (0, 0)),
            pl.BlockSpec(lmask.shape, lambda s: (0, 0)),
            pl.BlockSpec((R, dk), smap(lambda r, hi: (rb + r, sq + hi))),
            pl.BlockSpec((R, dk), smap(lambda r, hi: (rb + r, sk + hi))),
            pl.BlockSpec((R, dv), smap(lambda r, hi: (rb + r, sv + hi))),
            pl.BlockSpec((R, dk), smap(lambda r, hi: (rb + r, hi))),
            pl.BlockSpec(cmat_s.shape, lambda s: (0, 0)),
            pl.BlockSpec((None, nb, None, dk, dv), smap(lambda r, hi: (layer, r, hi, 0, 0))),
        ] + [pl.BlockSpec(memory_space=pl.ANY)] * len(carried),
        out_specs=[
            pl.BlockSpec((C, wv), pmap(lambda c: (c, 0))),
            pl.BlockSpec((None, heads, dk, dv), pmap(lambda c: (c // nt, 0, 0, 0))),
            pl.BlockSpec((R, dv), smap(lambda r, hi: (r, hi))),
            pl.BlockSpec((None, nb, None, dk, dv), smap(lambda r, hi: (layer, r, hi, 0, 0))),
        ],
        out_shape=[
            jax.ShapeDtypeStruct((batch * seq, heads * dv), F32),
            jax.ShapeDtypeStruct((batch, heads, dk, dv), F32),
            jax.ShapeDtypeStruct((dbatch * dseq, heads * dv), F32),
            jax.ShapeDtypeStruct(state.shape, F32),
        ],
        input_output_aliases={12: 3} if carried else {},
        scratch_shapes=[pltpu.VMEM((heads, dk, dv), F32)],
        compiler_params=_cparams(("arbitrary",)),
        name="gla",
    )(h, h, h, la, cmat_p, lmask, h, h, h, la, cmat_s, state, *carried)


def _window_means_minus_input(ext, u, pos, lead):
    n, pw = u.shape
    gw = pw // len(POOL_WINDOWS)
    outs = []
    for g, w in enumerate(POOL_WINDOWS):
        cols = slice(g * gw, (g + 1) * gw)
        p = ext[:, cols]
        span = 1
        while span < w:
            p = p + pltpu.roll(p, span, axis=0)
            span *= 2
        cnt = jnp.minimum(w, pos + 1).astype(F32)
        outs.append(p[lead:lead + n] / cnt - u[:, cols])
    return outs


def _mix_tail_rows(o_ref, pooled, on_ref, r_ref, ga_ref, gb_ref, x_ref, *weight_refs, alpha):
    tm = o_ref.shape[0]
    step = tm // POST_ROW_SPLITS
    for s in range(POST_ROW_SPLITS):
        rows = slice(s * step, (s + 1) * step)
        o_ref[rows, :] = _mix_tail([p[rows] for p in pooled], on_ref[rows, :], r_ref[rows, :],
                                   ga_ref[rows, :], gb_ref[rows, :], x_ref[rows, :], *weight_refs, alpha)


def _mix_tail(pooled, on, r, ga, gb, x, hg_ref, wgo_ref, pw_ref, ps_ref, wo_ref, g_ref, b_ref, alpha):
    ya = jnp.concatenate(
        [_dot(p.astype(BF16), pw_ref[g]) for g, p in enumerate(pooled)], axis=1) * ps_ref[...]
    yb = _dot((on * hg_ref[...] * (r * jax.nn.sigmoid(r))).astype(BF16), wgo_ref[...])
    merged = jax.nn.sigmoid(ga) * ya + jax.nn.sigmoid(gb) * yb
    m = _dot(merged.astype(BF16), wo_ref[...])
    return _layer_norm(alpha * x + m, g_ref[...], b_ref[...])


def _post_prompt_kernel(on_ref, r_ref, ga_ref, gb_ref, u_ref, x_ref, hg_ref, wgo_ref, pw_ref, ps_ref,
                        wo_ref, g_ref, b_ref, o_ref, npool_ref, halo_ref, *, tiles_per_seq, alpha):
    i = pl.program_id(0)
    tm = u_ref.shape[0]
    lead = halo_ref.shape[0]
    ti = i % tiles_per_seq

    @pl.when(ti == 0)
    def _():
        halo_ref[...] = jnp.zeros_like(halo_ref)

    u = u_ref[...]
    ext = jnp.concatenate([halo_ref[...], u], axis=0)
    halo_ref[...] = u[tm - lead:, :]

    @pl.when(ti == tiles_per_seq - 1)
    def _():
        npool_ref[...] = u[tm - POOL_BUF:, :]

    pos = ti * tm + lax.broadcasted_iota(jnp.int32, (tm, 1), 0)
    pooled = _window_means_minus_input(ext, u, pos, lead)
    _mix_tail_rows(o_ref, pooled, on_ref, r_ref, ga_ref, gb_ref, x_ref,
                   hg_ref, wgo_ref, pw_ref, ps_ref, wo_ref, g_ref, b_ref, alpha=alpha)


def _post_sample_kernel(on_ref, r_ref, ga_ref, gb_ref, u_ref, x_ref, st_ref, hg_ref, wgo_ref, pw_ref,
                        ps_ref, wo_ref, g_ref, b_ref, carried_ref, o_ref, npool_ref, ext_ref,
                        *, seq, start, alpha):
    del carried_ref
    tm, pw = u_ref.shape
    nb = tm // seq
    lead = ext_ref.shape[1] - seq
    u = u_ref[...]
    u3 = u.reshape(nb, seq, pw)
    st = st_ref[...]
    ext_ref[:, 0:lead - POOL_BUF, :] = jnp.zeros((nb, lead - POOL_BUF, pw), F32)
    ext_ref[:, lead - POOL_BUF:lead, :] = st
    ext_ref[:, lead:, :] = u3
    e3 = ext_ref[...]
    npool_ref[...] = e3[:, lead + seq - POOL_BUF:, :]
    p = e3.reshape(nb * (lead + seq), pw)
    gw = pw // len(POOL_WINDOWS)
    pos = start + lax.broadcasted_iota(jnp.int32, (nb, seq, 1), 1)
    pooled = []
    for g, w in enumerate(POOL_WINDOWS):
        cols = slice(g * gw, (g + 1) * gw)
        pg = p[:, cols]
        span = 1
        while span < w:
            pg = pg + pltpu.roll(pg, span, axis=0)
            span *= 2
        win = pg.reshape(nb, lead + seq, gw)[:, lead:, :]
        cnt = jnp.minimum(w, pos + 1).astype(F32)
        pooled.append((win / cnt - u3[:, :, cols]).reshape(tm, gw))
    _mix_tail_rows(o_ref, pooled, on_ref, r_ref, ga_ref, gb_ref, x_ref,
                   hg_ref, wgo_ref, pw_ref, ps_ref, wo_ref, g_ref, b_ref, alpha=alpha)


def _const_spec(shape, index_map):
    return pl.BlockSpec(shape, index_map, pipeline_mode=pl.Buffered(1))


def _post_weight_specs(layer, ln_idx, D, GV, groups, gw, ogw, n):
    z = (0,) * (n - 1)
    return [
        _const_spec((None, 1, GV), lambda *a: (layer, 0, 0)),
        _const_spec((None, GV, D), lambda *a: (layer, 0, 0)),
        _const_spec((None, groups, gw, ogw), lambda *a: (layer, 0, 0, 0)),
        _const_spec((None, 1, D), lambda *a: (layer, 0, 0)),
        _const_spec((None, D, D), lambda *a: (layer, 0, 0)),
        _const_spec((None, None, 1, D), lambda *a: (layer, ln_idx, 0, 0)),
        _const_spec((None, None, 1, D), lambda *a: (layer, ln_idx, 0, 0)),
    ]


def _post_prompt(on, h, x, weights, layer, ln_idx, alpha, seq, tm, cols):
    TP, GV = on.shape
    T, D = x.shape
    hg, wgo, pw, ps, wo, ln_g, ln_b = weights
    groups, gw, ogw = pw.shape[1:]
    PW = groups * gw
    tiles_per_seq = seq // tm
    r_b, ga_b, gb_b, u_b = cols["r"] // GV, cols["ga"] // D, cols["gb"] // D, cols["u"] // PW
    lead = 2 * V7X_SUBLANES
    return pl.pallas_call(
        functools.partial(_post_prompt_kernel, tiles_per_seq=tiles_per_seq, alpha=alpha),
        grid=(TP // tm,),
        in_specs=[
            pl.BlockSpec((tm, GV), lambda i: (i, 0)),
            pl.BlockSpec((tm, GV), lambda i: (i, r_b)),
            pl.BlockSpec((tm, D), lambda i: (i, ga_b)),
            pl.BlockSpec((tm, D), lambda i: (i, gb_b)),
            pl.BlockSpec((tm, PW), lambda i: (i, u_b)),
            pl.BlockSpec((tm, D), lambda i: (i, 0)),
        ] + _post_weight_specs(layer, ln_idx, D, GV, groups, gw, ogw, 1),
        out_specs=[
            pl.BlockSpec((tm, D), lambda i: (i, 0)),
            pl.BlockSpec((None, POOL_BUF, PW), lambda i: (i // tiles_per_seq, 0, 0)),
        ],
        out_shape=[
            jax.ShapeDtypeStruct((T, D), F32),
            jax.ShapeDtypeStruct((TP // seq, POOL_BUF, PW), F32),
        ],
        scratch_shapes=[pltpu.VMEM((lead, PW), F32)],
        compiler_params=_cparams(("arbitrary",)),
        name="post_prompt",
    )(on, h, h, h, h, x, hg, wgo, pw, ps, wo, ln_g, ln_b)


def _post_sample(on, h, x, x2, state_pool, weights, layer, ln_idx, alpha, row0, seq, tm, cols):
    TS, GV = on.shape
    D = x.shape[1]
    hg, wgo, pw, ps, wo, ln_g, ln_b = weights
    groups, gw, ogw = pw.shape[1:]
    PW = groups * gw
    nb = tm // seq
    batch = TS // seq
    rb = row0 // tm
    r_b, ga_b, gb_b, u_b = cols["r"] // GV, cols["ga"] // D, cols["gb"] // D, cols["u"] // PW
    lead = 2 * V7X_SUBLANES
    return pl.pallas_call(
        functools.partial(_post_sample_kernel, seq=seq, start=PAST_LEN, alpha=alpha),
        grid=(TS // tm,),
        in_specs=[
            pl.BlockSpec((tm, GV), lambda i: (i, 0)),
            pl.BlockSpec((tm, GV), lambda i: (rb + i, r_b)),
            pl.BlockSpec((tm, D), lambda i: (rb + i, ga_b)),
            pl.BlockSpec((tm, D), lambda i: (rb + i, gb_b)),
            pl.BlockSpec((tm, PW), lambda i: (rb + i, u_b)),
            pl.BlockSpec((tm, D), lambda i: (rb + i, 0)),
            pl.BlockSpec((None, nb, POOL_BUF, PW), lambda i: (layer, i, 0, 0)),
        ] + _post_weight_specs(layer, ln_idx, D, GV, groups, gw, ogw, 1) + [
            pl.BlockSpec(memory_space=pl.ANY),
        ],
        out_specs=[
            pl.BlockSpec((tm, D), lambda i: (rb + i, 0)),
            pl.BlockSpec((nb, POOL_BUF, PW), lambda i: (i, 0, 0)),
        ],
        out_shape=[
            jax.ShapeDtypeStruct(x2.shape, F32),
            jax.ShapeDtypeStruct((batch, POOL_BUF, PW), F32),
        ],
        input_output_aliases={14: 0},
        scratch_shapes=[pltpu.VMEM((nb, lead + seq, PW), F32)],
        compiler_params=_cparams(("arbitrary",)),
        name="post_sample",
    )(on, h, h, h, h, x, state_pool, hg, wgo, pw, ps, wo, ln_g, ln_b, x2)


def _tiles(T, TP, seq_p, F, NH):
    def fit(n, t):
        while n % t:
            t //= 2
        return t
    return dict(
        ffn_tm=fit(np.gcd(T, TP), 1024), ffn_tf=fit(F, 512),
        proj_tm=fit(np.gcd(T, TP), 1024), proj_tn=fit(NH, 1024),
        post_tm=fit(seq_p, 256), post_sample_tm=fit(np.gcd(T - TP, TP), 128),
    )


def kernel(x_prompt, x_sample, state_pool, state_gla, ln_g, ln_b, w_ffn_in, w_ffn_out, w_in, pool_w,
           pool_scale, a_up, a_bias, head_g, w_gla_out, w_out):
    B, SEQ, D = x_prompt.shape
    DB, DSEQ, _ = x_sample.shape
    L = w_in.shape[0]
    PW = state_pool.shape[-1]
    H, DK, DV = state_gla.shape[2:]
    GK, GV = H * DK, H * DV
    RANK = a_up.shape[1]
    F = w_ffn_out.shape[2]
    TP, TS = B * SEQ, DB * DSEQ
    T = TP + TS
    alpha = float((2 * L) ** 0.25)
    assert SEQ % GLA_CHUNK_ROWS == 0 and TP % V7X_LANES == 0 and V7X_LANES % DSEQ == 0
    assert DSEQ == V7X_SUBLANES and PAST_LEN >= POOL_BUF

    seg = (PW + 2 * GK, GV, RANK, GV + 2 * D)
    assert sum(seg) == w_in.shape[2]
    w_in_b = w_in.astype(BF16)
    aup = jnp.pad(a_up, ((0, 0), (0, V7X_LANES - RANK), (0, 0))).astype(BF16)
    abias = a_bias.reshape(L, 1, GK)
    ln_g4 = ln_g.reshape(L, 3, 1, D)
    ln_b4 = ln_b.reshape(L, 3, 1, D)
    post_w = (head_g.reshape(L, 1, GV), w_gla_out.astype(BF16), pool_w.astype(BF16),
              pool_scale.reshape(L, 1, D), w_out.astype(BF16), ln_g4, ln_b4)
    c_v = GV + 2 * D
    c_u = c_v + GV
    cols = dict(r=0, ga=GV, gb=GV + D, v=c_v, u=c_u, q=c_u + PW, k=c_u + PW + GK)
    ts = _tiles(T, TP, SEQ, F, w_in.shape[2] - RANK)
    assert GV == D and c_v % DV == 0 and c_u % PW == 0 and cols["q"] % DK == 0

    ffn = functools.partial(_ffn_ln, w_ffn_in=w_ffn_in, w_ffn_out=w_ffn_out, ln_g=ln_g4, ln_b=ln_b4,
                            alpha=alpha, tm=ts["ffn_tm"], tf=ts["ffn_tf"])

    xs = [x_prompt.reshape(TP, D), x_sample.reshape(TS, D)]
    pool_p, gla_p, pool_s, gla_s = [], [], [], None
    for l in range(L):
        (x,) = ffn(xs, [T], layer=l, which=0, ln_idx=0)
        h, la = _proj(x, w_in_b, aup, abias, l, ts["proj_tm"], ts["proj_tn"], seg)
        on_p, s_p, on_s, gla_s = _gla(h, la, state_gla, gla_s, l, B, SEQ, DB, DSEQ, H, DK, DV,
                                      cols["q"], cols["k"], cols["v"])
        x2, np_p = _post_prompt(on_p, h, x, post_w, l, 1, alpha, SEQ, ts["post_tm"], cols)
        x2, np_s = _post_sample(on_s, h, x, x2, state_pool, post_w, l, 1, alpha, TP, DSEQ,
                                ts["post_sample_tm"], cols)
        xs = ffn([x2], [T] if l + 1 < L else [TP, TS], layer=l, which=1, ln_idx=2)
        pool_p.append(np_p)
        gla_p.append(s_p)
        pool_s.append(np_s)
    return (xs[0].reshape(B, SEQ, D), xs[1].reshape(DB, DSEQ, D), jnp.stack(pool_p),
            jnp.stack(gla_p), jnp.stack(pool_s), gla_s)
```

```python
import functools

import numpy as np
import jax
import jax.numpy as jnp
from jax import lax
from jax.experimental import pallas as pl
from jax.experimental.pallas import tpu as pltpu

POOL_WINDOWS = (2, 4, 8, 16)
POOL_BUF = max(POOL_WINDOWS) - 1
GLA_TAU = 16.0
LN_EPS = 1e-5
HEAD_NORM_EPS = 1e-6
PAST_LEN = 16384

V7X_LANES = 128
V7X_SUBLANES = 8
V7X_VMEM_BYTES = 64 * 1024 * 1024
V7X_VMEM_LIMIT_BYTES = V7X_VMEM_BYTES - 1024 * 1024

F32 = jnp.float32
BF16 = jnp.bfloat16

GLA_SUB = 4
GLA_CHUNK_ROWS = 128
GLA_SAMPLE_ROWS = 64
LOG2_E = 1.4426950408889634


def _cparams(sem):
    return pltpu.CompilerParams(dimension_semantics=sem, vmem_limit_bytes=V7X_VMEM_LIMIT_BYTES)


def _layer_norm(y, g, b):
    mu = jnp.mean(y, axis=-1, keepdims=True)
    yc = y - mu
    var = jnp.mean(yc * yc, axis=-1, keepdims=True)
    return yc * lax.rsqrt(var + LN_EPS) * g + b


def _dot(a, b):
    return jnp.dot(a, b, preferred_element_type=F32)


def _dot_nt(a, b):
    return lax.dot_general(a, b, (((1,), (1,)), ((), ())), preferred_element_type=F32)


def _split3(x):
    p0 = x.astype(BF16)
    r1 = x - p0.astype(F32)
    p1 = r1.astype(BF16)
    p2 = (r1 - p1.astype(F32)).astype(BF16)
    return p0, p1, p2


FFN_DOWN_CHUNKS = 4


def _ffn_ln_kernel(*refs, alpha, n_carried, cast_weights):
    refs = list(refs)
    x_ref, wg_ref, wu_ref, wo_ref, g_ref, b_ref = [refs.pop(0) for _ in range(6)]
    del refs[:n_carried]
    o_ref = refs.pop(0)
    wb_refs = [refs.pop(0) for _ in range(3)] if cast_weights else None
    (xb_ref,) = refs
    j = pl.program_id(1)

    def ff_step(first):
        wg, wu, wo = wg_ref[...], wu_ref[...], wo_ref[...]
        if cast_weights:
            wg, wu, wo = wg.astype(BF16), wu.astype(BF16), wo.astype(BF16)
            for w_ref, w in zip(wb_refs, (wg, wu, wo)):
                w_ref[...] = w
        if first:
            xb = x_ref[...].astype(BF16)
            xb_ref[...] = xb
        else:
            xb = xb_ref[...]
        gate = _dot(xb, wg)
        up = _dot(xb, wu)
        act = (gate * jax.nn.sigmoid(gate) * up).astype(BF16)
        cw = o_ref.shape[1] // FFN_DOWN_CHUNKS
        for c in range(FFN_DOWN_CHUNKS):
            cs = slice(c * cw, (c + 1) * cw)
            if first:
                o_ref[:, cs] = _dot(act, wo[:, cs])
            else:
                o_ref[:, cs] += _dot(act, wo[:, cs])

    pl.when(j == 0)(lambda: ff_step(True))
    pl.when(j > 0)(lambda: ff_step(False))

    @pl.when(j == pl.num_programs(1) - 1)
    def _():
        y = alpha * x_ref[...] + 0.5 * o_ref[...]
        o_ref[...] = _layer_norm(y, g_ref[...], b_ref[...])


def _ffn_ln(xs, out_rows, w_ffn_in, w_ffn_out, ln_g, ln_b, layer, which, ln_idx, alpha, tm, tf):
    D = xs[0].shape[1]
    F = w_ffn_out.shape[2]
    ln_specs = [pl.BlockSpec((None, None, 1, D), lambda i, j: (layer, ln_idx, 0, 0))] * 2
    xb_scratch = pltpu.VMEM((tm, D), BF16)

    def tile_owner(rows_list):
        return [(a, t) for a, rows in enumerate(rows_list) for t in range(rows // tm)]
    runs = []
    for (sa, st), (da, dt) in zip(tile_owner([x.shape[0] for x in xs]), tile_owner(out_rows)):
        if runs and runs[-1][0] == sa and runs[-1][2] == da:
            runs[-1][4] += 1
        else:
            runs.append([sa, st, da, dt, 1])

    tf0 = tf // 2
    nf0 = F // tf0
    sa, st, da, dt, _ = runs[0]
    y0, wg_b, wu_b, wo_b = pl.pallas_call(
        functools.partial(_ffn_ln_kernel, alpha=alpha, n_carried=0, cast_weights=True),
        grid=(1, nf0),
        in_specs=[
            pl.BlockSpec((tm, D), lambda i, j: (st, 0), pipeline_mode=pl.Buffered(1)),
            pl.BlockSpec((None, None, D, tf0), lambda i, j: (layer, which, 0, j)),
            pl.BlockSpec((None, None, D, tf0), lambda i, j: (layer, which, 0, j + nf0)),
            pl.BlockSpec((None, None, tf0, D), lambda i, j: (layer, which, j, 0)),
        ] + ln_specs,
        out_specs=[
            pl.BlockSpec((tm, D), lambda i, j: (dt, 0)),
            pl.BlockSpec((D, tf0), lambda i, j: (0, j)),
            pl.BlockSpec((D, tf0), lambda i, j: (0, j)),
            pl.BlockSpec((tf0, D), lambda i, j: (j, 0)),
        ],
        out_shape=[
            jax.ShapeDtypeStruct((out_rows[da], D), F32),
            jax.ShapeDtypeStruct((D, F), BF16),
            jax.ShapeDtypeStruct((D, F), BF16),
            jax.ShapeDtypeStruct((F, D), BF16),
        ],
        scratch_shapes=[xb_scratch],
        compiler_params=_cparams(("arbitrary", "arbitrary")),
        name="ffn_ln_first",
    )(xs[sa], w_ffn_in, w_ffn_in, w_ffn_out, ln_g, ln_b)
    outs = {da: y0}
    runs[0][1] += 1
    runs[0][3] += 1
    runs[0][4] -= 1

    nf = F // tf
    for sa, st, da, dt, n in runs:
        if n == 0:
            continue
        carried = [outs[da]] if da in outs else []
        outs[da] = pl.pallas_call(
            functools.partial(_ffn_ln_kernel, alpha=alpha, n_carried=len(carried), cast_weights=False),
            grid=(n, nf),
            in_specs=[
                pl.BlockSpec((tm, D), lambda i, j, st=st: (i + st, 0)),
                pl.BlockSpec((D, tf), lambda i, j: (0, j)),
                pl.BlockSpec((D, tf), lambda i, j: (0, j)),
                pl.BlockSpec((tf, D), lambda i, j: (j, 0)),
            ] + ln_specs + [pl.BlockSpec(memory_space=pl.ANY)] * len(carried),
            out_specs=pl.BlockSpec((tm, D), lambda i, j, dt=dt: (i + dt, 0)),
            out_shape=jax.ShapeDtypeStruct((out_rows[da], D), F32),
            input_output_aliases={6: 0} if carried else {},
            scratch_shapes=[xb_scratch],
            compiler_params=_cparams(("arbitrary", "arbitrary")),
            name="ffn_ln",
        )(xs[sa], wg_b, wu_b, wo_b, ln_g, ln_b, *carried)
    return [outs[a] for a in range(len(out_rows))]


def _proj_row_prologue(x_ref, xb_ref, wa, aup_ref, ab_ref, la_ref):
    xb = x_ref[...].astype(BF16)
    xb_ref[...] = xb
    a_lo = _dot(xb, wa)
    z = _dot(a_lo.astype(BF16), aup_ref[...]) + ab_ref[...]
    la_ref[...] = (jnp.minimum(z, 0.0) - jnp.log(1.0 + jnp.exp(-jnp.abs(z)))) / GLA_TAU


def _proj_first_kernel(x_ref, wsrc_ref, wnext_ref, walo_ref, aup_ref, ab_ref, h_ref, la_ref, wm_ref,
                       wa_ref, xb_ref, *, n_shift, rank):
    j = pl.program_id(1)
    tn = wsrc_ref.shape[1]

    @pl.when(j == 0)
    def _():
        lane = lax.broadcasted_iota(jnp.int32, walo_ref.shape, 1)
        wa = jnp.where(lane < rank, walo_ref[...].astype(F32), 0.0).astype(BF16)
        wa_ref[...] = wa
        _proj_row_prologue(x_ref, xb_ref, wa, aup_ref, ab_ref, la_ref)

    def emit(w):
        wm_ref[...] = w
        h_ref[...] = _dot(xb_ref[...], w)

    @pl.when(j < n_shift)
    def _():
        both = jnp.concatenate([wsrc_ref[...], wnext_ref[...]], axis=1).astype(F32)
        emit(pltpu.roll(both, both.shape[1] - rank, axis=1)[:, :tn].astype(BF16))

    @pl.when(j >= n_shift)
    def _():
        emit(wsrc_ref[...])


def _proj_kernel(x_ref, wm_ref, wa_ref, aup_ref, ab_ref, h_in_ref, la_in_ref, h_ref, la_ref, xb_ref):
    del h_in_ref, la_in_ref
    j = pl.program_id(1)

    @pl.when(j == 0)
    def _():
        _proj_row_prologue(x_ref, xb_ref, wa_ref[...], aup_ref, ab_ref, la_ref)
        h_ref[...] = _dot(x_ref[...].astype(BF16), wm_ref[...])

    @pl.when(j > 0)
    def _():
        h_ref[...] = _dot(xb_ref[...], wm_ref[...])


def _proj(x, w_in, aup, abias, layer, tm, tn, seg):
    T, D = x.shape
    GK = aup.shape[2]
    head, mid, rank, tail = seg
    lo0 = head + mid
    NH = lo0 + tail
    tn0 = tn // 2
    assert lo0 % tn0 == 0 and head % tn0 == 0 and tail % tn0 == 0 and lo0 % V7X_LANES == 0
    n_shift, n_al = tail // tn0, lo0 // tn0
    per_tile = tn0 // V7X_LANES

    def src_block(j):
        return jnp.where(j < n_shift, n_al + j, (j - n_shift + head // tn0) % n_al)

    def next_block(j):
        return (n_al + jnp.minimum(j, n_shift - 1) + 1) * per_tile

    const = lambda i, j: (layer, 0, 0)
    h0, la0, wm, wa = pl.pallas_call(
        functools.partial(_proj_first_kernel, n_shift=n_shift, rank=rank),
        grid=(1, NH // tn0),
        in_specs=[
            pl.BlockSpec((tm, D), lambda i, j: (0, 0), pipeline_mode=pl.Buffered(1)),
            pl.BlockSpec((None, D, tn0), lambda i, j: (layer, 0, src_block(j))),
            pl.BlockSpec((None, D, V7X_LANES), lambda i, j: (layer, 0, next_block(j))),
            pl.BlockSpec((None, D, V7X_LANES), lambda i, j: (layer, 0, lo0 // V7X_LANES)),
            pl.BlockSpec((None, V7X_LANES, GK), const),
            pl.BlockSpec((None, 1, GK), const),
        ],
        out_specs=[
            pl.BlockSpec((tm, tn0), lambda i, j: (0, j)),
            pl.BlockSpec((tm, GK), lambda i, j: (0, 0)),
            pl.BlockSpec((D, tn0), lambda i, j: (0, j)),
            pl.BlockSpec((D, V7X_LANES), lambda i, j: (0, 0)),
        ],
        out_shape=[
            jax.ShapeDtypeStruct((T, NH), F32), jax.ShapeDtypeStruct((T, GK), F32),
            jax.ShapeDtypeStruct((D, NH), BF16), jax.ShapeDtypeStruct((D, V7X_LANES), BF16),
        ],
        scratch_shapes=[pltpu.VMEM((tm, D), BF16)],
        compiler_params=_cparams(("arbitrary", "arbitrary")),
        name="proj_first",
    )(x, w_in, w_in, w_in, aup, abias)

    return pl.pallas_call(
        _proj_kernel,
        grid=(T // tm - 1, NH // tn),
        in_specs=[
            pl.BlockSpec((tm, D), lambda i, j: (i + 1, 0)),
            pl.BlockSpec((D, tn), lambda i, j: (0, j)),
            pl.BlockSpec((D, V7X_LANES), lambda i, j: (0, 0)),
            pl.BlockSpec((None, V7X_LANES, GK), const),
            pl.BlockSpec((None, 1, GK), const),
            pl.BlockSpec(memory_space=pl.ANY),
            pl.BlockSpec(memory_space=pl.ANY),
        ],
        out_specs=[
            pl.BlockSpec((tm, tn), lambda i, j: (i + 1, j)),
            pl.BlockSpec((tm, GK), lambda i, j: (i + 1, 0)),
        ],
        out_shape=[jax.ShapeDtypeStruct((T, NH), F32), jax.ShapeDtypeStruct((T, GK), F32)],
        input_output_aliases={5: 0, 6: 1},
        scratch_shapes=[pltpu.VMEM((tm, D), BF16)],
        compiler_params=_cparams(("parallel", "arbitrary")),
        name="proj",
    )(x, wm, wa, aup, abias, h0, la0)


def _stack01(mats, dtype):
    return jnp.asarray(np.concatenate(mats, axis=0).astype(np.float32), dtype=dtype)


def _cumsum_mat(rows):
    i = np.arange(rows)[:, None]
    j = np.arange(rows)[None, :]
    return _stack01([j <= i], BF16)


def _seq_cumsum_mats(rows, seq):
    i = np.arange(rows)[:, None]
    j = np.arange(rows)[None, :]
    same = i // seq == j // seq
    return _stack01([same & (j <= i), same], BF16)


def _level_masks(rows, groups):
    i = np.arange(rows)[:, None]
    j = np.arange(rows)[None, :]
    out = []
    for grp in groups:
        half = grp // 2
        out.append((i // grp == j // grp) & (i % grp >= half) & (j % grp < half))
    return _stack01(out, F32)


def _intra_chunk_scores(qs, k, b, groups, lmask_ref, sub):
    C = qs.shape[0]
    A = jnp.zeros((C, C), F32)
    for n, grp in enumerate(groups):
        e = jnp.concatenate(
            [jnp.exp2(-jnp.abs(b[s:s + grp] - b[s + grp // 2 - 1:s + grp // 2])) for s in range(0, C, grp)],
            axis=0)
        A = A + _dot_nt((qs * e).astype(BF16), (k * e).astype(BF16)) * lmask_ref[n * C:(n + 1) * C, :]
    row_in_sub = lax.broadcasted_iota(jnp.int32, (C, 1), 0) & (sub - 1)
    delta = lax.broadcasted_iota(jnp.int32, (C, C), 0) - lax.broadcasted_iota(jnp.int32, (C, C), 1)
    for d in range(sub):
        if d == 0:
            t = qs * k
        else:
            kd = pltpu.roll(k, d, axis=0)
            bd = pltpu.roll(b, d, axis=0)
            t = qs * kd * jnp.exp2(jnp.minimum(b - bd, 0.0))
        red = jnp.sum(t, axis=1, keepdims=True)
        red = jnp.where(row_in_sub >= d, red, 0.0)
        A = A + jnp.where(delta == d, red, 0.0)
    return A


def _head_norm(o):
    return o * lax.rsqrt(jnp.mean(o * o, axis=-1, keepdims=True) + HEAD_NORM_EPS)


def _gla_prompt_step(t, nt, q_ref, k_ref, v_ref, la_ref, cmat_ref, lmask_ref, o_ref, sout_ref, s_ref,
                     *, scale, sub, groups):
    C = q_ref.shape[0]
    HB, DK, DV = s_ref.shape

    @pl.when(t == 0)
    def _():
        s_ref[...] = jnp.zeros_like(s_ref)

    cm = cmat_ref[...]
    p0, p1, p2 = _split3(la_ref[...] * LOG2_E)
    b_all = _dot(cm, p0) + _dot(cm, p1) + _dot(cm, p2)
    for hh in range(HB):
        kc = slice(hh * DK, (hh + 1) * DK)
        vc = slice(hh * DV, (hh + 1) * DV)
        qs = q_ref[:, kc] * scale
        k = k_ref[:, kc]
        vb = v_ref[:, vc].astype(BF16)
        b = b_all[:, kc]

        A = _intra_chunk_scores(qs, k, b, groups, lmask_ref, sub)
        S = s_ref[hh]
        o = _dot((qs * jnp.exp2(b)).astype(BF16), S.astype(BF16)) + _dot(A.astype(BF16), vb)
        o_ref[:, vc] = _head_norm(o)

        bl = b[C - 1:C, :]
        ku_t = (k * jnp.exp2(bl - b)).T
        d_s = _dot(ku_t.astype(BF16), vb)
        a_t = jnp.exp2(jnp.broadcast_to(bl, (V7X_LANES, DK))).T
        for n in range(DV // V7X_LANES):
            sl = slice(n * V7X_LANES, (n + 1) * V7X_LANES)
            s_ref[hh, :, sl] = a_t * S[:, sl] + d_s[:, sl]

    @pl.when(t == nt - 1)
    def _():
        sout_ref[...] = s_ref[...]


def _gla_sample_step(q_ref, k_ref, v_ref, la_ref, cmat_ref, sin_ref, o_ref, sout_ref, *, scale, seq):
    R, DK = q_ref.shape
    DV = v_ref.shape[1]
    nb = R // seq
    pad = V7X_LANES - R

    qs = q_ref[...] * scale
    k = k_ref[...]
    vb = v_ref[...].astype(BF16)
    cm = cmat_ref[...]
    p0, p1, p2 = _split3(la_ref[...] * LOG2_E)
    ball = _dot(cm, p0) + _dot(cm, p1) + _dot(cm, p2)
    b = ball[0:R]
    bl = ball[R:2 * R]

    A = _intra_chunk_scores(qs, k, b, (), None, seq)
    o_intra = _dot(A.astype(BF16), vb)
    qo = qs * jnp.exp2(b)
    ku = k * jnp.exp2(bl - b)
    if pad:
        ku = jnp.concatenate([ku, jnp.zeros((pad, DK), F32)], axis=0)
        vb = jnp.concatenate([vb, jnp.zeros((pad, DV), BF16)], axis=0)
    ku_t = ku.T
    lane_seq = lax.broadcasted_iota(jnp.int32, ku_t.shape, 1) // seq
    for i in range(nb):
        rows = slice(i * seq, (i + 1) * seq)
        S = sin_ref[i]
        o = _dot(qo[rows].astype(BF16), S.astype(BF16)) + o_intra[rows]
        o_ref[rows, :] = _head_norm(o)
        d_s = _dot(jnp.where(lane_seq == i, ku_t, 0.0).astype(BF16), vb)
        a_t = jnp.exp2(jnp.broadcast_to(bl[i * seq:i * seq + 1, :], (V7X_LANES, DK))).T
        for n in range(DV // V7X_LANES):
            sl = slice(n * V7X_LANES, (n + 1) * V7X_LANES)
            sout_ref[i, :, sl] = a_t * S[:, sl] + d_s[:, sl]


def _gla_kernel(*refs, n_prompt, n_sample, nt, n_carried, prompt_kw, sample_kw):
    refs = list(refs)
    p_in = [refs.pop(0) for _ in range(6)]
    s_in = [refs.pop(0) for _ in range(6)]
    del refs[:n_carried]
    p_out = [refs.pop(0) for _ in range(2)]
    s_out = [refs.pop(0) for _ in range(2)]
    (s_ref,) = refs
    s = pl.program_id(0)

    def guarded(n_steps, fn):
        if n_steps == max(n_prompt, n_sample):
            fn()
        else:
            pl.when(s < n_steps)(fn)

    guarded(n_prompt, lambda: _gla_prompt_step(s % nt, nt, *p_in, *p_out, s_ref, **prompt_kw))
    guarded(n_sample, lambda: _gla_sample_step(*s_in, *s_out, **sample_kw))


def _gla(h, la, state, new_state, layer, batch, seq, dbatch, dseq, heads, dk, dv, q_off, k_off, v_off):
    C = GLA_CHUNK_ROWS
    nt = seq // C
    groups = []
    g = C
    while g > GLA_SUB:
        groups.append(g)
        g //= 2
    cmat_p = _cumsum_mat(C)
    lmask = _level_masks(C, groups)
    wk, wv = heads * dk, heads * dv
    assert q_off % wk == 0 and k_off % wk == 0 and v_off % wv == 0
    pq, pk, pv = q_off // wk, k_off // wk, v_off // wv
    n_prompt = batch * nt

    R = GLA_SAMPLE_ROWS
    nb = R // dseq
    rb = batch * seq // R
    cmat_s = _seq_cumsum_mats(R, dseq)
    sq, sk, sv = q_off // dk, k_off // dk, v_off // dv
    n_sample = (dbatch // nb) * heads
    n_steps = max(n_prompt, n_sample)

    def pmap(fn):
        return lambda s: fn(jnp.minimum(s, n_prompt - 1))

    def smap(fn):
        return lambda s: fn(jnp.minimum(s, n_sample - 1) // heads, jnp.minimum(s, n_sample - 1) % heads)

    carried = [] if new_state is None else [new_state]
    return pl.pallas_call(
        functools.partial(
            _gla_kernel, n_prompt=n_prompt, n_sample=n_sample, nt=nt, n_carried=len(carried),
            prompt_kw=dict(scale=float(dk) ** -0.5, sub=GLA_SUB, groups=tuple(groups)),
            sample_kw=dict(scale=float(dk) ** -0.5, seq=dseq)),
        grid=(n_steps,),
        in_specs=[
            pl.BlockSpec((C, wk), pmap(lambda c: (c, pq))),
            pl.BlockSpec((C, wk), pmap(lambda c: (c, pk))),
            pl.BlockSpec((C, wv), pmap(lambda c: (c, pv))),
            pl.BlockSpec((C, wk), pmap(lambda c: (c, 0))),
            pl.BlockSpec(cmat_p.shape, lambda s: (0, 0)),
            pl.BlockSpec(lmask.shape, lambda s: (0, 0)),
            pl.BlockSpec((R, dk), smap(lambda r, hi: (rb + r, sq + hi))),
            pl.BlockSpec((R, dk), smap(lambda r, hi: (rb + r, sk + hi))),
            pl.BlockSpec((R, dv), smap(lambda r, hi: (rb + r, sv + hi))),
            pl.BlockSpec((R, dk), smap(lambda r, hi: (rb + r, hi))),
            pl.BlockSpec(cmat_s.shape, lambda s: (0, 0)),
            pl.BlockSpec((None, nb, None, dk, dv), smap(lambda r, hi: (layer, r, hi, 0, 0))),
        ] + [pl.BlockSpec(memory_space=pl.ANY)] * len(carried),
        out_specs=[
            pl.BlockSpec((C, wv), pmap(lambda c: (c, 0))),
            pl.BlockSpec((None, heads, dk, dv), pmap(lambda c: (c // nt, 0, 0, 0))),
            pl.BlockSpec((R, dv), smap(lambda r, hi: (r, hi))),
            pl.BlockSpec((None, nb, None, dk, dv), smap(lambda r, hi: (layer, r, hi, 0, 0))),
        ],
        out_shape=[
            jax.ShapeDtypeStruct((batch * seq, heads * dv), F32),
            jax.ShapeDtypeStruct((batch, heads, dk, dv), F32),
            jax.ShapeDtypeStruct((dbatch * dseq, heads * dv), F32),
            jax.ShapeDtypeStruct(state.shape, F32),
        ],
        input_output_aliases={12: 3} if carried else {},
        scratch_shapes=[pltpu.VMEM((heads, dk, dv), F32)],
        compiler_params=_cparams(("arbitrary",)),
        name="gla",
    )(h, h, h, la, cmat_p, lmask, h, h, h, la, cmat_s, state, *carried)


def _window_means_minus_input(ext, u, pos, lead):
    n, pw = u.shape
    gw = pw // len(POOL_WINDOWS)
    outs = []
    for g, w in enumerate(POOL_WINDOWS):
        cols = slice(g * gw, (g + 1) * gw)
        p = ext[:, cols]
        span = 1
        while span < w:
            p = p + pltpu.roll(p, span, axis=0)
            span *= 2
        cnt = jnp.minimum(w, pos + 1).astype(F32)
        outs.append(p[lead:lead + n] / cnt - u[:, cols])
    return outs


def _mix_tail(pooled, on, r, ga, gb, x, hg_ref, wgo_ref, pw_ref, ps_ref, wo_ref, g_ref, b_ref, alpha):
    ya = jnp.concatenate(
        [_dot(p.astype(BF16), pw_ref[g]) for g, p in enumerate(pooled)], axis=1) * ps_ref[...]
    yb = _dot((on * hg_ref[...] * (r * jax.nn.sigmoid(r))).astype(BF16), wgo_ref[...])
    merged = jax.nn.sigmoid(ga) * ya + jax.nn.sigmoid(gb) * yb
    m = _dot(merged.astype(BF16), wo_ref[...])
    return _layer_norm(alpha * x + m, g_ref[...], b_ref[...])


def _post_prompt_kernel(on_ref, r_ref, ga_ref, gb_ref, u_ref, x_ref, hg_ref, wgo_ref, pw_ref, ps_ref,
                        wo_ref, g_ref, b_ref, o_ref, npool_ref, halo_ref, *, tiles_per_seq, alpha):
    i = pl.program_id(0)
    tm = u_ref.shape[0]
    lead = halo_ref.shape[0]
    ti = i % tiles_per_seq

    @pl.when(ti == 0)
    def _():
        halo_ref[...] = jnp.zeros_like(halo_ref)

    u = u_ref[...]
    ext = jnp.concatenate([halo_ref[...], u], axis=0)
    halo_ref[...] = u[tm - lead:, :]

    @pl.when(ti == tiles_per_seq - 1)
    def _():
        npool_ref[...] = u[tm - POOL_BUF:, :]

    pos = ti * tm + lax.broadcasted_iota(jnp.int32, (tm, 1), 0)
    pooled = _window_means_minus_input(ext, u, pos, lead)
    o_ref[...] = _mix_tail(pooled, on_ref[...], r_ref[...], ga_ref[...], gb_ref[...], x_ref[...],
                           hg_ref, wgo_ref, pw_ref, ps_ref, wo_ref, g_ref, b_ref, alpha)


def _post_sample_kernel(on_ref, r_ref, ga_ref, gb_ref, u_ref, x_ref, st_ref, hg_ref, wgo_ref, pw_ref,
                        ps_ref, wo_ref, g_ref, b_ref, carried_ref, o_ref, npool_ref, ext_ref,
                        *, seq, start, alpha):
    del carried_ref
    tm, pw = u_ref.shape
    nb = tm // seq
    lead = ext_ref.shape[1] - seq
    u = u_ref[...]
    u3 = u.reshape(nb, seq, pw)
    st = st_ref[...]
    ext_ref[:, 0:lead - POOL_BUF, :] = jnp.zeros((nb, lead - POOL_BUF, pw), F32)
    ext_ref[:, lead - POOL_BUF:lead, :] = st
    ext_ref[:, lead:, :] = u3
    e3 = ext_ref[...]
    npool_ref[...] = e3[:, lead + seq - POOL_BUF:, :]
    p = e3.reshape(nb * (lead + seq), pw)
    gw = pw // len(POOL_WINDOWS)
    pos = start + lax.broadcasted_iota(jnp.int32, (nb, seq, 1), 1)
    pooled = []
    for g, w in enumerate(POOL_WINDOWS):
        cols = slice(g * gw, (g + 1) * gw)
        pg = p[:, cols]
        span = 1
        while span < w:
            pg = pg + pltpu.roll(pg, span, axis=0)
            span *= 2
        win = pg.reshape(nb, lead + seq, gw)[:, lead:, :]
        cnt = jnp.minimum(w, pos + 1).astype(F32)
        pooled.append((win / cnt - u3[:, :, cols]).reshape(tm, gw))
    o_ref[...] = _mix_tail(pooled, on_ref[...], r_ref[...], ga_ref[...], gb_ref[...], x_ref[...],
                           hg_ref, wgo_ref, pw_ref, ps_ref, wo_ref, g_ref, b_ref, alpha)


def _const_spec(shape, index_map):
    return pl.BlockSpec(shape, index_map, pipeline_mode=pl.Buffered(1))


def _post_weight_specs(layer, ln_idx, D, GV, groups, gw, ogw, n):
    z = (0,) * (n - 1)
    return [
        _const_spec((None, 1, GV), lambda *a: (layer, 0, 0)),
        _const_spec((None, GV, D), lambda *a: (layer, 0, 0)),
        _const_spec((None, groups, gw, ogw), lambda *a: (layer, 0, 0, 0)),
        _const_spec((None, 1, D), lambda *a: (layer, 0, 0)),
        _const_spec((None, D, D), lambda *a: (layer, 0, 0)),
        _const_spec((None, None, 1, D), lambda *a: (layer, ln_idx, 0, 0)),
        _const_spec((None, None, 1, D), lambda *a: (layer, ln_idx, 0, 0)),
    ]


def _post_prompt(on, h, x, weights, layer, ln_idx, alpha, seq, tm, cols):
    TP, GV = on.shape
    T, D = x.shape
    hg, wgo, pw, ps, wo, ln_g, ln_b = weights
    groups, gw, ogw = pw.shape[1:]
    PW = groups * gw
    tiles_per_seq = seq // tm
    r_b, ga_b, gb_b, u_b = cols["r"] // GV, cols["ga"] // D, cols["gb"] // D, cols["u"] // PW
    lead = 2 * V7X_SUBLANES
    return pl.pallas_call(
        functools.partial(_post_prompt_kernel, tiles_per_seq=tiles_per_seq, alpha=alpha),
        grid=(TP // tm,),
        in_specs=[
            pl.BlockSpec((tm, GV), lambda i: (i, 0)),
            pl.BlockSpec((tm, GV), lambda i: (i, r_b)),
            pl.BlockSpec((tm, D), lambda i: (i, ga_b)),
            pl.BlockSpec((tm, D), lambda i: (i, gb_b)),
            pl.BlockSpec((tm, PW), lambda i: (i, u_b)),
            pl.BlockSpec((tm, D), lambda i: (i, 0)),
        ] + _post_weight_specs(layer, ln_idx, D, GV, groups, gw, ogw, 1),
        out_specs=[
            pl.BlockSpec((tm, D), lambda i: (i, 0)),
            pl.BlockSpec((None, POOL_BUF, PW), lambda i: (i // tiles_per_seq, 0, 0)),
        ],
        out_shape=[
            jax.ShapeDtypeStruct((T, D), F32),
            jax.ShapeDtypeStruct((TP // seq, POOL_BUF, PW), F32),
        ],
        scratch_shapes=[pltpu.VMEM((lead, PW), F32)],
        compiler_params=_cparams(("arbitrary",)),
        name="post_prompt",
    )(on, h, h, h, h, x, hg, wgo, pw, ps, wo, ln_g, ln_b)


def _post_sample(on, h, x, x2, state_pool, weights, layer, ln_idx, alpha, row0, seq, tm, cols):
    TS, GV = on.shape
    D = x.shape[1]
    hg, wgo, pw, ps, wo, ln_g, ln_b = weights
    groups, gw, ogw = pw.shape[1:]
    PW = groups * gw
    nb = tm // seq
    batch = TS // seq
    rb = row0 // tm
    r_b, ga_b, gb_b, u_b = cols["r"] // GV, cols["ga"] // D, cols["gb"] // D, cols["u"] // PW
    lead = 2 * V7X_SUBLANES
    return pl.pallas_call(
        functools.partial(_post_sample_kernel, seq=seq, start=PAST_LEN, alpha=alpha),
        grid=(TS // tm,),
        in_specs=[
            pl.BlockSpec((tm, GV), lambda i: (i, 0)),
            pl.BlockSpec((tm, GV), lambda i: (rb + i, r_b)),
            pl.BlockSpec((tm, D), lambda i: (rb + i, ga_b)),
            pl.BlockSpec((tm, D), lambda i: (rb + i, gb_b)),
            pl.BlockSpec((tm, PW), lambda i: (rb + i, u_b)),
            pl.BlockSpec((tm, D), lambda i: (rb + i, 0)),
            pl.BlockSpec((None, nb, POOL_BUF, PW), lambda i: (layer, i, 0, 0)),
        ] + _post_weight_specs(layer, ln_idx, D, GV, groups, gw, ogw, 1) + [
            pl.BlockSpec(memory_space=pl.ANY),
        ],
        out_specs=[
            pl.BlockSpec((tm, D), lambda i: (rb + i, 0)),
            pl.BlockSpec((nb, POOL_BUF, PW), lambda i: (i, 0, 0)),
        ],
        out_shape=[
            jax.ShapeDtypeStruct(x2.shape, F32),
            jax.ShapeDtypeStruct((batch, POOL_BUF, PW), F32),
        ],
        input_output_aliases={14: 0},
        scratch_shapes=[pltpu.VMEM((nb, lead + seq, PW), F32)],
        compiler_params=_cparams(("arbitrary",)),
        name="post_sample",
    )(on, h, h, h, h, x, state_pool, hg, wgo, pw, ps, wo, ln_g, ln_b, x2)


def _tiles(T, TP, seq_p, F, NH):
    def fit(n, t):
        while n % t:
            t //= 2
        return t
    return dict(
        ffn_tm=fit(np.gcd(T, TP), 1024), ffn_tf=fit(F, 512),
        proj_tm=fit(np.gcd(T, TP), 1024), proj_tn=fit(NH, 1024),
        post_tm=fit(seq_p, 256), post_sample_tm=fit(np.gcd(T - TP, TP), 256),
    )


def kernel(x_prompt, x_sample, state_pool, state_gla, ln_g, ln_b, w_ffn_in, w_ffn_out, w_in, pool_w,
           pool_scale, a_up, a_bias, head_g, w_gla_out, w_out):
    B, SEQ, D = x_prompt.shape
    DB, DSEQ, _ = x_sample.shape
    L = w_in.shape[0]
    PW = state_pool.shape[-1]
    H, DK, DV = state_gla.shape[2:]
    GK, GV = H * DK, H * DV
    RANK = a_up.shape[1]
    F = w_ffn_out.shape[2]
    TP, TS = B * SEQ, DB * DSEQ
    T = TP + TS
    alpha = float((2 * L) ** 0.25)
    assert SEQ % GLA_CHUNK_ROWS == 0 and TP % V7X_LANES == 0 and V7X_LANES % DSEQ == 0
    assert DSEQ == V7X_SUBLANES and PAST_LEN >= POOL_BUF

    seg = (PW + 2 * GK, GV, RANK, GV + 2 * D)
    assert sum(seg) == w_in.shape[2]
    w_in_b = w_in.astype(BF16)
    aup = jnp.pad(a_up, ((0, 0), (0, V7X_LANES - RANK), (0, 0))).astype(BF16)
    abias = a_bias.reshape(L, 1, GK)
    ln_g4 = ln_g.reshape(L, 3, 1, D)
    ln_b4 = ln_b.reshape(L, 3, 1, D)
    post_w = (head_g.reshape(L, 1, GV), w_gla_out.astype(BF16), pool_w.astype(BF16),
              pool_scale.reshape(L, 1, D), w_out.astype(BF16), ln_g4, ln_b4)
    c_v = GV + 2 * D
    c_u = c_v + GV
    cols = dict(r=0, ga=GV, gb=GV + D, v=c_v, u=c_u, q=c_u + PW, k=c_u + PW + GK)
    ts = _tiles(T, TP, SEQ, F, w_in.shape[2] - RANK)
    assert GV == D and c_v % DV == 0 and c_u % PW == 0 and cols["q"] % DK == 0

    ffn = functools.partial(_ffn_ln, w_ffn_in=w_ffn_in, w_ffn_out=w_ffn_out, ln_g=ln_g4, ln_b=ln_b4,
                            alpha=alpha, tm=ts["ffn_tm"], tf=ts["ffn_tf"])

    xs = [x_prompt.reshape(TP, D), x_sample.reshape(TS, D)]
    pool_p, gla_p, pool_s, gla_s = [], [], [], None
    for l in range(L):
        (x,) = ffn(xs, [T], layer=l, which=0, ln_idx=0)
        h, la = _proj(x, w_in_b, aup, abias, l, ts["proj_tm"], ts["proj_tn"], seg)
        on_p, s_p, on_s, gla_s = _gla(h, la, state_gla, gla_s, l, B, SEQ, DB, DSEQ, H, DK, DV,
                                      cols["q"], cols["k"], cols["v"])
        x2, np_p = _post_prompt(on_p, h, x, post_w, l, 1, alpha, SEQ, ts["post_tm"], cols)
        x2, np_s = _post_sample(on_s, h, x, x2, state_pool, post_w, l, 1, alpha, TP, DSEQ,
                                ts["post_sample_tm"], cols)
        xs = ffn([x2], [T] if l + 1 < L else [TP, TS], layer=l, which=1, ln_idx=2)
        pool_p.append(np_p)
        gla_p.append(s_p)
        pool_s.append(np_s)
    return (xs[0].reshape(B, SEQ, D), xs[1].reshape(DB, DSEQ, D), jnp.stack(pool_p),
            jnp.stack(gla_p), jnp.stack(pool_s), gla_s)
```

```python
import functools

import numpy as np
import jax
import jax.numpy as jnp
from jax import lax
from jax.experimental import pallas as pl
from jax.experimental.pallas import tpu as pltpu

POOL_WINDOWS = (2, 4, 8, 16)
POOL_BUF = max(POOL_WINDOWS) - 1
GLA_TAU = 16.0
LN_EPS = 1e-5
HEAD_NORM_EPS = 1e-6
PAST_LEN = 16384

V7X_LANES = 128
V7X_SUBLANES = 8
V7X_VMEM_BYTES = 64 * 1024 * 1024
V7X_VMEM_LIMIT_BYTES = V7X_VMEM_BYTES - 1024 * 1024

F32 = jnp.float32
BF16 = jnp.bfloat16

GLA_SUB = 4
GLA_CHUNK_ROWS = 128
GLA_SAMPLE_ROWS = 64
LOG2_E = 1.4426950408889634


def _cparams(sem):
    return pltpu.CompilerParams(dimension_semantics=sem, vmem_limit_bytes=V7X_VMEM_LIMIT_BYTES)


def _layer_norm(y, g, b):
    mu = jnp.mean(y, axis=-1, keepdims=True)
    yc = y - mu
    var = jnp.mean(yc * yc, axis=-1, keepdims=True)
    return yc * lax.rsqrt(var + LN_EPS) * g + b


def _dot(a, b):
    return jnp.dot(a, b, preferred_element_type=F32)


def _dot_nt(a, b):
    return lax.dot_general(a, b, (((1,), (1,)), ((), ())), preferred_element_type=F32)


def _split3(x):
    p0 = x.astype(BF16)
    r1 = x - p0.astype(F32)
    p1 = r1.astype(BF16)
    p2 = (r1 - p1.astype(F32)).astype(BF16)
    return p0, p1, p2


FFN_DOWN_CHUNKS = 4


def _ffn_ln_kernel(*refs, alpha, n_carried, cast_weights):
    refs = list(refs)
    x_ref, wg_ref, wu_ref, wo_ref, g_ref, b_ref = [refs.pop(0) for _ in range(6)]
    del refs[:n_carried]
    o_ref = refs.pop(0)
    wb_refs = [refs.pop(0) for _ in range(3)] if cast_weights else None
    (xb_ref,) = refs
    j = pl.program_id(1)

    def ff_step(first):
        wg, wu, wo = wg_ref[...], wu_ref[...], wo_ref[...]
        if cast_weights:
            wg, wu, wo = wg.astype(BF16), wu.astype(BF16), wo.astype(BF16)
            for w_ref, w in zip(wb_refs, (wg, wu, wo)):
                w_ref[...] = w
        if first:
            xb = x_ref[...].astype(BF16)
            xb_ref[...] = xb
        else:
            xb = xb_ref[...]
        gate = _dot(xb, wg)
        up = _dot(xb, wu)
        act = (gate * jax.nn.sigmoid(gate) * up).astype(BF16)
        cw = o_ref.shape[1] // FFN_DOWN_CHUNKS
        for c in range(FFN_DOWN_CHUNKS):
            cs = slice(c * cw, (c + 1) * cw)
            if first:
                o_ref[:, cs] = _dot(act, wo[:, cs])
            else:
                o_ref[:, cs] += _dot(act, wo[:, cs])

    pl.when(j == 0)(lambda: ff_step(True))
    pl.when(j > 0)(lambda: ff_step(False))

    @pl.when(j == pl.num_programs(1) - 1)
    def _():
        y = alpha * x_ref[...] + 0.5 * o_ref[...]
        o_ref[...] = _layer_norm(y, g_ref[...], b_ref[...])


def _ffn_ln(xs, out_rows, w_ffn_in, w_ffn_out, ln_g, ln_b, layer, which, ln_idx, alpha, tm, tf):
    D = xs[0].shape[1]
    F = w_ffn_out.shape[2]
    ln_specs = [pl.BlockSpec((None, None, 1, D), lambda i, j: (layer, ln_idx, 0, 0))] * 2
    xb_scratch = pltpu.VMEM((tm, D), BF16)

    def tile_owner(rows_list):
        return [(a, t) for a, rows in enumerate(rows_list) for t in range(rows // tm)]
    runs = []
    for (sa, st), (da, dt) in zip(tile_owner([x.shape[0] for x in xs]), tile_owner(out_rows)):
        if runs and runs[-1][0] == sa and runs[-1][2] == da:
            runs[-1][4] += 1
        else:
            runs.append([sa, st, da, dt, 1])

    tf0 = tf // 2
    nf0 = F // tf0
    sa, st, da, dt, _ = runs[0]
    y0, wg_b, wu_b, wo_b = pl.pallas_call(
        functools.partial(_ffn_ln_kernel, alpha=alpha, n_carried=0, cast_weights=True),
        grid=(1, nf0),
        in_specs=[
            pl.BlockSpec((tm, D), lambda i, j: (st, 0), pipeline_mode=pl.Buffered(1)),
            pl.BlockSpec((None, None, D, tf0), lambda i, j: (layer, which, 0, j)),
            pl.BlockSpec((None, None, D, tf0), lambda i, j: (layer, which, 0, j + nf0)),
            pl.BlockSpec((None, None, tf0, D), lambda i, j: (layer, which, j, 0)),
        ] + ln_specs,
        out_specs=[
            pl.BlockSpec((tm, D), lambda i, j: (dt, 0)),
            pl.BlockSpec((D, tf0), lambda i, j: (0, j)),
            pl.BlockSpec((D, tf0), lambda i, j: (0, j)),
            pl.BlockSpec((tf0, D), lambda i, j: (j, 0)),
        ],
        out_shape=[
            jax.ShapeDtypeStruct((out_rows[da], D), F32),
            jax.ShapeDtypeStruct((D, F), BF16),
            jax.ShapeDtypeStruct((D, F), BF16),
            jax.ShapeDtypeStruct((F, D), BF16),
        ],
        scratch_shapes=[xb_scratch],
        compiler_params=_cparams(("arbitrary", "arbitrary")),
        name="ffn_ln_first",
    )(xs[sa], w_ffn_in, w_ffn_in, w_ffn_out, ln_g, ln_b)
    outs = {da: y0}
    runs[0][1] += 1
    runs[0][3] += 1
    runs[0][4] -= 1

    nf = F // tf
    for sa, st, da, dt, n in runs:
        if n == 0:
            continue
        carried = [outs[da]] if da in outs else []
        outs[da] = pl.pallas_call(
            functools.partial(_ffn_ln_kernel, alpha=alpha, n_carried=len(carried), cast_weights=False),
            grid=(n, nf),
            in_specs=[
                pl.BlockSpec((tm, D), lambda i, j, st=st: (i + st, 0)),
                pl.BlockSpec((D, tf), lambda i, j: (0, j)),
                pl.BlockSpec((D, tf), lambda i, j: (0, j)),
                pl.BlockSpec((tf, D), lambda i, j: (j, 0)),
            ] + ln_specs + [pl.BlockSpec(memory_space=pl.ANY)] * len(carried),
            out_specs=pl.BlockSpec((tm, D), lambda i, j, dt=dt: (i + dt, 0)),
            out_shape=jax.ShapeDtypeStruct((out_rows[da], D), F32),
            input_output_aliases={6: 0} if carried else {},
            scratch_shapes=[xb_scratch],
            compiler_params=_cparams(("arbitrary", "arbitrary")),
            name="ffn_ln",
        )(xs[sa], wg_b, wu_b, wo_b, ln_g, ln_b, *carried)
    return [outs[a] for a in range(len(out_rows))]


def _proj_row_prologue(x_ref, xb_ref, wa, aup_ref, ab_ref, la_ref):
    xb = x_ref[...].astype(BF16)
    xb_ref[...] = xb
    a_lo = _dot(xb, wa)
    z = _dot(a_lo.astype(BF16), aup_ref[...]) + ab_ref[...]
    la_ref[...] = (jnp.minimum(z, 0.0) - jnp.log(1.0 + jnp.exp(-jnp.abs(z)))) / GLA_TAU


def _proj_first_kernel(x_ref, wsrc_ref, wnext_ref, walo_ref, aup_ref, ab_ref, h_ref, la_ref, wm_ref,
                       wa_ref, xb_ref, *, n_shift, rank):
    j = pl.program_id(1)
    tn = wsrc_ref.shape[1]

    @pl.when(j == 0)
    def _():
        lane = lax.broadcasted_iota(jnp.int32, walo_ref.shape, 1)
        wa = jnp.where(lane < rank, walo_ref[...].astype(F32), 0.0).astype(BF16)
        wa_ref[...] = wa
        _proj_row_prologue(x_ref, xb_ref, wa, aup_ref, ab_ref, la_ref)

    def emit(w):
        wm_ref[...] = w
        h_ref[...] = _dot(xb_ref[...], w)

    @pl.when(j < n_shift)
    def _():
        both = jnp.concatenate([wsrc_ref[...], wnext_ref[...]], axis=1).astype(F32)
        emit(pltpu.roll(both, both.shape[1] - rank, axis=1)[:, :tn].astype(BF16))

    @pl.when(j >= n_shift)
    def _():
        emit(wsrc_ref[...])


def _proj_kernel(x_ref, wm_ref, wa_ref, aup_ref, ab_ref, h_in_ref, la_in_ref, h_ref, la_ref, xb_ref):
    del h_in_ref, la_in_ref
    j = pl.program_id(1)

    @pl.when(j == 0)
    def _():
        _proj_row_prologue(x_ref, xb_ref, wa_ref[...], aup_ref, ab_ref, la_ref)
        h_ref[...] = _dot(x_ref[...].astype(BF16), wm_ref[...])

    @pl.when(j > 0)
    def _():
        h_ref[...] = _dot(xb_ref[...], wm_ref[...])


def _proj(x, w_in, aup, abias, layer, tm, tn, seg):
    T, D = x.shape
    GK = aup.shape[2]
    head, mid, rank, tail = seg
    lo0 = head + mid
    NH = lo0 + tail
    tn0 = tn // 2
    assert lo0 % tn0 == 0 and head % tn0 == 0 and tail % tn0 == 0 and lo0 % V7X_LANES == 0
    n_shift, n_al = tail // tn0, lo0 // tn0
    per_tile = tn0 // V7X_LANES

    def src_block(j):
        return jnp.where(j < n_shift, n_al + j, (j - n_shift + head // tn0) % n_al)

    def next_block(j):
        return (n_al + jnp.minimum(j, n_shift - 1) + 1) * per_tile

    const = lambda i, j: (layer, 0, 0)
    h0, la0, wm, wa = pl.pallas_call(
        functools.partial(_proj_first_kernel, n_shift=n_shift, rank=rank),
        grid=(1, NH // tn0),
        in_specs=[
            pl.BlockSpec((tm, D), lambda i, j: (0, 0), pipeline_mode=pl.Buffered(1)),
            pl.BlockSpec((None, D, tn0), lambda i, j: (layer, 0, src_block(j))),
            pl.BlockSpec((None, D, V7X_LANES), lambda i, j: (layer, 0, next_block(j))),
            pl.BlockSpec((None, D, V7X_LANES), lambda i, j: (layer, 0, lo0 // V7X_LANES)),
            pl.BlockSpec((None, V7X_LANES, GK), const),
            pl.BlockSpec((None, 1, GK), const),
        ],
        out_specs=[
            pl.BlockSpec((tm, tn0), lambda i, j: (0, j)),
            pl.BlockSpec((tm, GK), lambda i, j: (0, 0)),
            pl.BlockSpec((D, tn0), lambda i, j: (0, j)),
            pl.BlockSpec((D, V7X_LANES), lambda i, j: (0, 0)),
        ],
        out_shape=[
            jax.ShapeDtypeStruct((T, NH), F32), jax.ShapeDtypeStruct((T, GK), F32),
            jax.ShapeDtypeStruct((D, NH), BF16), jax.ShapeDtypeStruct((D, V7X_LANES), BF16),
        ],
        scratch_shapes=[pltpu.VMEM((tm, D), BF16)],
        compiler_params=_cparams(("arbitrary", "arbitrary")),
        name="proj_first",
    )(x, w_in, w_in, w_in, aup, abias)

    return pl.pallas_call(
        _proj_kernel,
        grid=(T // tm - 1, NH // tn),
        in_specs=[
            pl.BlockSpec((tm, D), lambda i, j: (i + 1, 0)),
            pl.BlockSpec((D, tn), lambda i, j: (0, j)),
            pl.BlockSpec((D, V7X_LANES), lambda i, j: (0, 0)),
            pl.BlockSpec((None, V7X_LANES, GK), const),
            pl.BlockSpec((None, 1, GK), const),
            pl.BlockSpec(memory_space=pl.ANY),
            pl.BlockSpec(memory_space=pl.ANY),
        ],
        out_specs=[
            pl.BlockSpec((tm, tn), lambda i, j: (i + 1, j)),
            pl.BlockSpec((tm, GK), lambda i, j: (i + 1, 0)),
        ],
        out_shape=[jax.ShapeDtypeStruct((T, NH), F32), jax.ShapeDtypeStruct((T, GK), F32)],
        input_output_aliases={5: 0, 6: 1},
        scratch_shapes=[pltpu.VMEM((tm, D), BF16)],
        compiler_params=_cparams(("parallel", "arbitrary")),
        name="proj",
    )(x, wm, wa, aup, abias, h0, la0)


def _stack01(mats, dtype):
    return jnp.asarray(np.concatenate(mats, axis=0).astype(np.float32), dtype=dtype)


def _cumsum_mat(rows):
    i = np.arange(rows)[:, None]
    j = np.arange(rows)[None, :]
    return _stack01([j <= i], BF16)


def _seq_cumsum_mats(rows, seq):
    i = np.arange(rows)[:, None]
    j = np.arange(rows)[None, :]
    same = i // seq == j // seq
    return _stack01([same & (j <= i), same], BF16)


def _level_masks(rows, groups):
    i = np.arange(rows)[:, None]
    j = np.arange(rows)[None, :]
    out = []
    for grp in groups:
        half = grp // 2
        out.append((i // grp == j // grp) & (i % grp >= half) & (j % grp < half))
    return _stack01(out, F32)


def _intra_chunk_scores(qs, k, b, groups, lmask_ref, sub):
    C = qs.shape[0]
    A = jnp.zeros((C, C), F32)
    for n, grp in enumerate(groups):
        e = jnp.concatenate(
            [jnp.exp2(-jnp.abs(b[s:s + grp] - b[s + grp // 2 - 1:s + grp // 2])) for s in range(0, C, grp)],
            axis=0)
        A = A + _dot_nt((qs * e).astype(BF16), (k * e).astype(BF16)) * lmask_ref[n * C:(n + 1) * C, :]
    row_in_sub = lax.broadcasted_iota(jnp.int32, (C, 1), 0) & (sub - 1)
    delta = lax.broadcasted_iota(jnp.int32, (C, C), 0) - lax.broadcasted_iota(jnp.int32, (C, C), 1)
    for d in range(sub):
        if d == 0:
            t = qs * k
        else:
            kd = pltpu.roll(k, d, axis=0)
            bd = pltpu.roll(b, d, axis=0)
            t = qs * kd * jnp.exp2(jnp.minimum(b - bd, 0.0))
        red = jnp.sum(t, axis=1, keepdims=True)
        red = jnp.where(row_in_sub >= d, red, 0.0)
        A = A + jnp.where(delta == d, red, 0.0)
    return A


def _head_norm(o):
    return o * lax.rsqrt(jnp.mean(o * o, axis=-1, keepdims=True) + HEAD_NORM_EPS)


def _gla_prompt_step(t, nt, q_ref, k_ref, v_ref, la_ref, cmat_ref, lmask_ref, o_ref, sout_ref, s_ref,
                     *, scale, sub, groups):
    C = q_ref.shape[0]
    HB, DK, DV = s_ref.shape

    @pl.when(t == 0)
    def _():
        s_ref[...] = jnp.zeros_like(s_ref)

    cm = cmat_ref[...]
    p0, p1, p2 = _split3(la_ref[...] * LOG2_E)
    b_all = _dot(cm, p0) + _dot(cm, p1) + _dot(cm, p2)
    for hh in range(HB):
        kc = slice(hh * DK, (hh + 1) * DK)
        vc = slice(hh * DV, (hh + 1) * DV)
        qs = q_ref[:, kc] * scale
        k = k_ref[:, kc]
        vb = v_ref[:, vc].astype(BF16)
        b = b_all[:, kc]

        A = _intra_chunk_scores(qs, k, b, groups, lmask_ref, sub)
        S = s_ref[hh]
        o = _dot((qs * jnp.exp2(b)).astype(BF16), S.astype(BF16)) + _dot(A.astype(BF16), vb)
        o_ref[:, vc] = _head_norm(o)

        bl = b[C - 1:C, :]
        ku_t = (k * jnp.exp2(bl - b)).T
        d_s = _dot(ku_t.astype(BF16), vb)
        a_t = jnp.exp2(jnp.broadcast_to(bl, (V7X_LANES, DK))).T
        for n in range(DV // V7X_LANES):
            sl = slice(n * V7X_LANES, (n + 1) * V7X_LANES)
            s_ref[hh, :, sl] = a_t * S[:, sl] + d_s[:, sl]

    @pl.when(t == nt - 1)
    def _():
        sout_ref[...] = s_ref[...]


def _gla_sample_step(q_ref, k_ref, v_ref, la_ref, cmat_ref, sin_ref, o_ref, sout_ref, *, scale, seq):
    R, DK = q_ref.shape
    DV = v_ref.shape[1]
    nb = R // seq
    pad = V7X_LANES - R

    qs = q_ref[...] * scale
    k = k_ref[...]
    vb = v_ref[...].astype(BF16)
    cm = cmat_ref[...]
    p0, p1, p2 = _split3(la_ref[...] * LOG2_E)
    ball = _dot(cm, p0) + _dot(cm, p1) + _dot(cm, p2)
    b = ball[0:R]
    bl = ball[R:2 * R]

    A = _intra_chunk_scores(qs, k, b, (), None, seq)
    o_intra = _dot(A.astype(BF16), vb)
    qo = qs * jnp.exp2(b)
    ku = k * jnp.exp2(bl - b)
    if pad:
        ku = jnp.concatenate([ku, jnp.zeros((pad, DK), F32)], axis=0)
        vb = jnp.concatenate([vb, jnp.zeros((pad, DV), BF16)], axis=0)
    ku_t = ku.T
    lane_seq = lax.broadcasted_iota(jnp.int32, ku_t.shape, 1) // seq
    for i in range(nb):
        rows = slice(i * seq, (i + 1) * seq)
        S = sin_ref[i]
        o = _dot(qo[rows].astype(BF16), S.astype(BF16)) + o_intra[rows]
        o_ref[rows, :] = _head_norm(o)
        d_s = _dot(jnp.where(lane_seq == i, ku_t, 0.0).astype(BF16), vb)
        a_t = jnp.exp2(jnp.broadcast_to(bl[i * seq:i * seq + 1, :], (V7X_LANES, DK))).T
        for n in range(DV // V7X_LANES):
            sl = slice(n * V7X_LANES, (n + 1) * V7X_LANES)
            sout_ref[i, :, sl] = a_t * S[:, sl] + d_s[:, sl]


def _gla_kernel(*refs, n_prompt, n_sample, nt, n_carried, prompt_kw, sample_kw):
    refs = list(refs)
    p_in = [refs.pop(0) for _ in range(6)]
    s_in = [refs.pop(0) for _ in range(6)]
    del refs[:n_carried]
    p_out = [refs.pop(0) for _ in range(2)]
    s_out = [refs.pop(0) for _ in range(2)]
    (s_ref,) = refs
    s = pl.program_id(0)

    def guarded(n_steps, fn):
        if n_steps == max(n_prompt, n_sample):
            fn()
        else:
            pl.when(s < n_steps)(fn)

    guarded(n_prompt, lambda: _gla_prompt_step(s % nt, nt, *p_in, *p_out, s_ref, **prompt_kw))
    guarded(n_sample, lambda: _gla_sample_step(*s_in, *s_out, **sample_kw))


def _gla(h, la, state, new_state, prompt_states, layer, batch, seq, dbatch, dseq, heads, dk, dv,
         q_off, k_off, v_off):
    C = GLA_CHUNK_ROWS
    nt = seq // C
    groups = []
    g = C
    while g > GLA_SUB:
        groups.append(g)
        g //= 2
    cmat_p = _cumsum_mat(C)
    lmask = _level_masks(C, groups)
    wk, wv = heads * dk, heads * dv
    assert q_off % wk == 0 and k_off % wk == 0 and v_off % wv == 0
    pq, pk, pv = q_off // wk, k_off // wk, v_off // wv
    n_prompt = batch * nt

    R = GLA_SAMPLE_ROWS
    nb = R // dseq
    rb = batch * seq // R
    cmat_s = _seq_cumsum_mats(R, dseq)
    sq, sk, sv = q_off // dk, k_off // dk, v_off // dv
    n_sample = (dbatch // nb) * heads
    n_steps = max(n_prompt, n_sample)

    def pmap(fn):
        return lambda s: fn(jnp.minimum(s, n_prompt - 1))

    def smap(fn):
        return lambda s: fn(jnp.minimum(s, n_sample - 1) // heads, jnp.minimum(s, n_sample - 1) % heads)

    carried = [] if new_state is None else [new_state, prompt_states]
    n_layers = state.shape[0]
    return pl.pallas_call(
        functools.partial(
            _gla_kernel, n_prompt=n_prompt, n_sample=n_sample, nt=nt, n_carried=len(carried),
            prompt_kw=dict(scale=float(dk) ** -0.5, sub=GLA_SUB, groups=tuple(groups)),
            sample_kw=dict(scale=float(dk) ** -0.5, seq=dseq)),
        grid=(n_steps,),
        in_specs=[
            pl.BlockSpec((C, wk), pmap(lambda c: (c, pq))),
            pl.BlockSpec((C, wk), pmap(lambda c: (c, pk))),
            pl.BlockSpec((C, wv), pmap(lambda c: (c, pv))),
            pl.BlockSpec((C, wk), pmap(lambda c: (c, 0))),
            pl.BlockSpec(cmat_p.shape, lambda s: (0, 0)),
            pl.BlockSpec(lmask.shape, lambda s: (0, 0)),
            pl.BlockSpec((R, dk), smap(lambda r, hi: (rb + r, sq + hi))),
            pl.BlockSpec((R, dk), smap(lambda r, hi: (rb + r, sk + hi))),
            pl.BlockSpec((R, dv), smap(lambda r, hi: (rb + r, sv + hi))),
            pl.BlockSpec((R, dk), smap(lambda r, hi: (rb + r, hi))),
            pl.BlockSpec(cmat_s.shape, lambda s: (0, 0)),
            pl.BlockSpec((None, nb, None, dk, dv), smap(lambda r, hi: (layer, r, hi, 0, 0))),
        ] + [pl.BlockSpec(memory_space=pl.ANY)] * len(carried),
        out_specs=[
            pl.BlockSpec((C, wv), pmap(lambda c: (c, 0))),
            pl.BlockSpec((None, None, heads, dk, dv), pmap(lambda c: (layer, c // nt, 0, 0, 0))),
            pl.BlockSpec((R, dv), smap(lambda r, hi: (r, hi))),
            pl.BlockSpec((None, nb, None, dk, dv), smap(lambda r, hi: (layer, r, hi, 0, 0))),
        ],
        out_shape=[
            jax.ShapeDtypeStruct((batch * seq, heads * dv), F32),
            jax.ShapeDtypeStruct((n_layers, batch, heads, dk, dv), F32),
            jax.ShapeDtypeStruct((dbatch * dseq, heads * dv), F32),
            jax.ShapeDtypeStruct(state.shape, F32),
        ],
        input_output_aliases={12: 3, 13: 1} if carried else {},
        scratch_shapes=[pltpu.VMEM((heads, dk, dv), F32)],
        compiler_params=_cparams(("arbitrary",)),
        name="gla",
    )(h, h, h, la, cmat_p, lmask, h, h, h, la, cmat_s, state, *carried)


def _window_means_minus_input(ext, u, pos, lead):
    n, pw = u.shape
    gw = pw // len(POOL_WINDOWS)
    outs = []
    for g, w in enumerate(POOL_WINDOWS):
        cols = slice(g * gw, (g + 1) * gw)
        p = ext[:, cols]
        span = 1
        while span < w:
            p = p + pltpu.roll(p, span, axis=0)
            span *= 2
        cnt = jnp.minimum(w, pos + 1).astype(F32)
        outs.append(p[lead:lead + n] / cnt - u[:, cols])
    return outs


def _mix_tail(pooled, on, r, ga, gb, x, hg_ref, wgo_ref, pw_ref, ps_ref, wo_ref, g_ref, b_ref, alpha):
    ya = jnp.concatenate(
        [_dot(p.astype(BF16), pw_ref[g]) for g, p in enumerate(pooled)], axis=1) * ps_ref[...]
    yb = _dot((on * hg_ref[...] * (r * jax.nn.sigmoid(r))).astype(BF16), wgo_ref[...])
    merged = jax.nn.sigmoid(ga) * ya + jax.nn.sigmoid(gb) * yb
    m = _dot(merged.astype(BF16), wo_ref[...])
    return _layer_norm(alpha * x + m, g_ref[...], b_ref[...])


def _post_prompt_kernel(on_ref, r_ref, ga_ref, gb_ref, u_ref, x_ref, hg_ref, wgo_ref, pw_ref, ps_ref,
                        wo_ref, g_ref, b_ref, o_ref, npool_ref, halo_ref, *, tiles_per_seq, alpha):
    i = pl.program_id(0)
    tm = u_ref.shape[0]
    lead = halo_ref.shape[0]
    ti = i % tiles_per_seq

    @pl.when(ti == 0)
    def _():
        halo_ref[...] = jnp.zeros_like(halo_ref)

    u = u_ref[...]
    ext = jnp.concatenate([halo_ref[...], u], axis=0)
    halo_ref[...] = u[tm - lead:, :]

    @pl.when(ti == tiles_per_seq - 1)
    def _():
        npool_ref[...] = u[tm - POOL_BUF:, :]

    pos = ti * tm + lax.broadcasted_iota(jnp.int32, (tm, 1), 0)
    pooled = _window_means_minus_input(ext, u, pos, lead)
    o_ref[...] = _mix_tail(pooled, on_ref[...], r_ref[...], ga_ref[...], gb_ref[...], x_ref[...],
                           hg_ref, wgo_ref, pw_ref, ps_ref, wo_ref, g_ref, b_ref, alpha)


def _post_sample_kernel(on_ref, r_ref, ga_ref, gb_ref, u_ref, x_ref, st_ref, hg_ref, wgo_ref, pw_ref,
                        ps_ref, wo_ref, g_ref, b_ref, *rest, seq, start, alpha):
    o_ref, npool_ref, ext_ref = rest[-3:]
    tm, pw = u_ref.shape
    nb = tm // seq
    lead = ext_ref.shape[1] - seq
    u = u_ref[...]
    u3 = u.reshape(nb, seq, pw)
    st = st_ref[...]
    ext_ref[:, 0:lead - POOL_BUF, :] = jnp.zeros((nb, lead - POOL_BUF, pw), F32)
    ext_ref[:, lead - POOL_BUF:lead, :] = st
    ext_ref[:, lead:, :] = u3
    e3 = ext_ref[...]
    npool_ref[...] = e3[:, lead + seq - POOL_BUF:, :]
    p = e3.reshape(nb * (lead + seq), pw)
    gw = pw // len(POOL_WINDOWS)
    pos = start + lax.broadcasted_iota(jnp.int32, (nb, seq, 1), 1)
    pooled = []
    for g, w in enumerate(POOL_WINDOWS):
        cols = slice(g * gw, (g + 1) * gw)
        pg = p[:, cols]
        span = 1
        while span < w:
            pg = pg + pltpu.roll(pg, span, axis=0)
            span *= 2
        win = pg.reshape(nb, lead + seq, gw)[:, lead:, :]
        cnt = jnp.minimum(w, pos + 1).astype(F32)
        pooled.append((win / cnt - u3[:, :, cols]).reshape(tm, gw))
    o_ref[...] = _mix_tail(pooled, on_ref[...], r_ref[...], ga_ref[...], gb_ref[...], x_ref[...],
                           hg_ref, wgo_ref, pw_ref, ps_ref, wo_ref, g_ref, b_ref, alpha)


def _const_spec(shape, index_map):
    return pl.BlockSpec(shape, index_map, pipeline_mode=pl.Buffered(1))


def _post_weight_specs(layer, ln_idx, D, GV, groups, gw, ogw, n):
    z = (0,) * (n - 1)
    return [
        _const_spec((None, 1, GV), lambda *a: (layer, 0, 0)),
        _const_spec((None, GV, D), lambda *a: (layer, 0, 0)),
        _const_spec((None, groups, gw, ogw), lambda *a: (layer, 0, 0, 0)),
        _const_spec((None, 1, D), lambda *a: (layer, 0, 0)),
        _const_spec((None, D, D), lambda *a: (layer, 0, 0)),
        _const_spec((None, None, 1, D), lambda *a: (layer, ln_idx, 0, 0)),
        _const_spec((None, None, 1, D), lambda *a: (layer, ln_idx, 0, 0)),
    ]


def _post_prompt(on, h, x, weights, layer, ln_idx, alpha, seq, tm, cols):
    TP, GV = on.shape
    T, D = x.shape
    hg, wgo, pw, ps, wo, ln_g, ln_b = weights
    groups, gw, ogw = pw.shape[1:]
    PW = groups * gw
    tiles_per_seq = seq // tm
    r_b, ga_b, gb_b, u_b = cols["r"] // GV, cols["ga"] // D, cols["gb"] // D, cols["u"] // PW
    lead = 2 * V7X_SUBLANES
    return pl.pallas_call(
        functools.partial(_post_prompt_kernel, tiles_per_seq=tiles_per_seq, alpha=alpha),
        grid=(TP // tm,),
        in_specs=[
            pl.BlockSpec((tm, GV), lambda i: (i, 0)),
            pl.BlockSpec((tm, GV), lambda i: (i, r_b)),
            pl.BlockSpec((tm, D), lambda i: (i, ga_b)),
            pl.BlockSpec((tm, D), lambda i: (i, gb_b)),
            pl.BlockSpec((tm, PW), lambda i: (i, u_b)),
            pl.BlockSpec((tm, D), lambda i: (i, 0)),
        ] + _post_weight_specs(layer, ln_idx, D, GV, groups, gw, ogw, 1),
        out_specs=[
            pl.BlockSpec((tm, D), lambda i: (i, 0)),
            pl.BlockSpec((None, POOL_BUF, PW), lambda i: (i // tiles_per_seq, 0, 0)),
        ],
        out_shape=[
            jax.ShapeDtypeStruct((T, D), F32),
            jax.ShapeDtypeStruct((TP // seq, POOL_BUF, PW), F32),
        ],
        scratch_shapes=[pltpu.VMEM((lead, PW), F32)],
        compiler_params=_cparams(("arbitrary",)),
        name="post_prompt",
    )(on, h, h, h, h, x, hg, wgo, pw, ps, wo, ln_g, ln_b)


def _post_sample(on, h, x, x2, new_pool, state_pool, weights, layer, ln_idx, alpha, row0, seq, tm, cols):
    carried = [x2] if new_pool is None else [x2, new_pool]
    TS, GV = on.shape
    D = x.shape[1]
    hg, wgo, pw, ps, wo, ln_g, ln_b = weights
    groups, gw, ogw = pw.shape[1:]
    PW = groups * gw
    nb = tm // seq
    batch = TS // seq
    rb = row0 // tm
    r_b, ga_b, gb_b, u_b = cols["r"] // GV, cols["ga"] // D, cols["gb"] // D, cols["u"] // PW
    lead = 2 * V7X_SUBLANES
    return pl.pallas_call(
        functools.partial(_post_sample_kernel, seq=seq, start=PAST_LEN, alpha=alpha),
        grid=(TS // tm,),
        in_specs=[
            pl.BlockSpec((tm, GV), lambda i: (i, 0)),
            pl.BlockSpec((tm, GV), lambda i: (rb + i, r_b)),
            pl.BlockSpec((tm, D), lambda i: (rb + i, ga_b)),
            pl.BlockSpec((tm, D), lambda i: (rb + i, gb_b)),
            pl.BlockSpec((tm, PW), lambda i: (rb + i, u_b)),
            pl.BlockSpec((tm, D), lambda i: (rb + i, 0)),
            pl.BlockSpec((None, nb, POOL_BUF, PW), lambda i: (layer, i, 0, 0)),
        ] + _post_weight_specs(layer, ln_idx, D, GV, groups, gw, ogw, 1) + [
            pl.BlockSpec(memory_space=pl.ANY),
        ] * len(carried),
        out_specs=[
            pl.BlockSpec((tm, D), lambda i: (rb + i, 0)),
            pl.BlockSpec((None, nb, POOL_BUF, PW), lambda i: (layer, i, 0, 0)),
        ],
        out_shape=[
            jax.ShapeDtypeStruct(x2.shape, F32),
            jax.ShapeDtypeStruct((state_pool.shape[0], batch, POOL_BUF, PW), F32),
        ],
        input_output_aliases={14 + n: n for n in range(len(carried))},
        scratch_shapes=[pltpu.VMEM((nb, lead + seq, PW), F32)],
        compiler_params=_cparams(("arbitrary",)),
        name="post_sample",
    )(on, h, h, h, h, x, state_pool, hg, wgo, pw, ps, wo, ln_g, ln_b, *carried)


def _tiles(T, TP, seq_p, F, NH):
    def fit(n, t):
        while n % t:
            t //= 2
        return t
    return dict(
        ffn_tm=fit(np.gcd(T, TP), 1024), ffn_tf=fit(F, 512),
        proj_tm=fit(np.gcd(T, TP), 1024), proj_tn=fit(NH, 1024),
        post_tm=fit(seq_p, 256), post_sample_tm=fit(np.gcd(T - TP, TP), 256),
    )


def kernel(x_prompt, x_sample, state_pool, state_gla, ln_g, ln_b, w_ffn_in, w_ffn_out, w_in, pool_w,
           pool_scale, a_up, a_bias, head_g, w_gla_out, w_out):
    B, SEQ, D = x_prompt.shape
    DB, DSEQ, _ = x_sample.shape
    L = w_in.shape[0]
    PW = state_pool.shape[-1]
    H, DK, DV = state_gla.shape[2:]
    GK, GV = H * DK, H * DV
    RANK = a_up.shape[1]
    F = w_ffn_out.shape[2]
    TP, TS = B * SEQ, DB * DSEQ
    T = TP + TS
    alpha = float((2 * L) ** 0.25)
    assert SEQ % GLA_CHUNK_ROWS == 0 and TP % V7X_LANES == 0 and V7X_LANES % DSEQ == 0
    assert DSEQ == V7X_SUBLANES and PAST_LEN >= POOL_BUF

    seg = (PW + 2 * GK, GV, RANK, GV + 2 * D)
    assert sum(seg) == w_in.shape[2]
    w_in_b = w_in.astype(BF16)
    aup = jnp.pad(a_up, ((0, 0), (0, V7X_LANES - RANK), (0, 0))).astype(BF16)
    abias = a_bias.reshape(L, 1, GK)
    ln_g4 = ln_g.reshape(L, 3, 1, D)
    ln_b4 = ln_b.reshape(L, 3, 1, D)
    post_w = (head_g.reshape(L, 1, GV), w_gla_out.astype(BF16), pool_w.astype(BF16),
              pool_scale.reshape(L, 1, D), w_out.astype(BF16), ln_g4, ln_b4)
    c_v = GV + 2 * D
    c_u = c_v + GV
    cols = dict(r=0, ga=GV, gb=GV + D, v=c_v, u=c_u, q=c_u + PW, k=c_u + PW + GK)
    ts = _tiles(T, TP, SEQ, F, w_in.shape[2] - RANK)
    assert GV == D and c_v % DV == 0 and c_u % PW == 0 and cols["q"] % DK == 0

    ffn = functools.partial(_ffn_ln, w_ffn_in=w_ffn_in, w_ffn_out=w_ffn_out, ln_g=ln_g4, ln_b=ln_b4,
                            alpha=alpha, tm=ts["ffn_tm"], tf=ts["ffn_tf"])

    xs = [x_prompt.reshape(TP, D), x_sample.reshape(TS, D)]
    pool_p, gla_p, pool_s, gla_s = [], None, None, None
    for l in range(L):
        (x,) = ffn(xs, [T], layer=l, which=0, ln_idx=0)
        h, la = _proj(x, w_in_b, aup, abias, l, ts["proj_tm"], ts["proj_tn"], seg)
        on_p, gla_p, on_s, gla_s = _gla(h, la, state_gla, gla_s, gla_p, l, B, SEQ, DB, DSEQ, H, DK, DV,
                                        cols["q"], cols["k"], cols["v"])
        x2, np_p = _post_prompt(on_p, h, x, post_w, l, 1, alpha, SEQ, ts["post_tm"], cols)
        x2, pool_s = _post_sample(on_s, h, x, x2, pool_s, state_pool, post_w, l, 1, alpha, TP, DSEQ,
                                  ts["post_sample_tm"], cols)
        xs = ffn([x2], [T] if l + 1 < L else [TP, TS], layer=l, which=1, ln_idx=2)
        pool_p.append(np_p)
    return (xs[0].reshape(B, SEQ, D), xs[1].reshape(DB, DSEQ, D), jnp.stack(pool_p),
            gla_p, pool_s, gla_s)
```

```python
import functools

import numpy as np
import jax
import jax.numpy as jnp
from jax import lax
from jax.experimental import pallas as pl
from jax.experimental.pallas import tpu as pltpu

POOL_WINDOWS = (2, 4, 8, 16)
POOL_BUF = max(POOL_WINDOWS) - 1
GLA_TAU = 16.0
LN_EPS = 1e-5
HEAD_NORM_EPS = 1e-6
PAST_LEN = 16384

V7X_LANES = 128
V7X_SUBLANES = 8
V7X_MXU_COLS = 256
V7X_VMEM_BYTES = 64 * 1024 * 1024
V7X_VMEM_LIMIT_BYTES = V7X_VMEM_BYTES - 1024 * 1024

F32 = jnp.float32
BF16 = jnp.bfloat16

GLA_SUB = 4
GLA_CHUNK_ROWS = 128
GLA_SAMPLE_ROWS = 64
LOG2_E = 1.4426950408889634


def _cparams(sem):
    return pltpu.CompilerParams(dimension_semantics=sem, vmem_limit_bytes=V7X_VMEM_LIMIT_BYTES)


def _layer_norm(y, g, b):
    mu = jnp.mean(y, axis=-1, keepdims=True)
    yc = y - mu
    var = jnp.mean(yc * yc, axis=-1, keepdims=True)
    return yc * lax.rsqrt(var + LN_EPS) * g + b


def _dot(a, b):
    return jnp.dot(a, b, preferred_element_type=F32)


def _dot_nt(a, b):
    return lax.dot_general(a, b, (((1,), (1,)), ((), ())), preferred_element_type=F32)


def _split3(x):
    p0 = x.astype(BF16)
    r1 = x - p0.astype(F32)
    p1 = r1.astype(BF16)
    p2 = (r1 - p1.astype(F32)).astype(BF16)
    return p0, p1, p2


FFN_DOWN_CHUNKS = 4
FFN_FF_SPLITS = 2


def _ffn_ln_kernel(*refs, alpha, n_carried, cast_weights):
    refs = list(refs)
    x_ref, wg_ref, wu_ref, wo_ref, g_ref, b_ref = [refs.pop(0) for _ in range(6)]
    del refs[:n_carried]
    o_ref = refs.pop(0)
    wb_refs = [refs.pop(0) for _ in range(3)] if cast_weights else None
    (xb_ref,) = refs
    j = pl.program_id(1)

    def ff_step(first):
        wg, wu, wo = wg_ref[...], wu_ref[...], wo_ref[...]
        if cast_weights:
            wg, wu, wo = wg.astype(BF16), wu.astype(BF16), wo.astype(BF16)
            for w_ref, w in zip(wb_refs, (wg, wu, wo)):
                w_ref[...] = w
        if first:
            xb = x_ref[...].astype(BF16)
            xb_ref[...] = xb
        else:
            xb = xb_ref[...]
        n_sub = max(1, min(FFN_FF_SPLITS, wg.shape[1] // V7X_MXU_COLS))
        fw = wg.shape[1] // n_sub
        acts = []
        for s in range(n_sub):
            fs = slice(s * fw, (s + 1) * fw)
            gate = _dot(xb, wg[:, fs])
            up = _dot(xb, wu[:, fs])
            acts.append((gate * jax.nn.sigmoid(gate) * up).astype(BF16))
        cw = o_ref.shape[1] // FFN_DOWN_CHUNKS
        for c in range(FFN_DOWN_CHUNKS):
            cs = slice(c * cw, (c + 1) * cw)
            down = sum(_dot(a, wo[s * fw:(s + 1) * fw, cs]) for s, a in enumerate(acts))
            if first:
                o_ref[:, cs] = down
            else:
                o_ref[:, cs] += down

    pl.when(j == 0)(lambda: ff_step(True))
    pl.when(j > 0)(lambda: ff_step(False))

    @pl.when(j == pl.num_programs(1) - 1)
    def _():
        y = alpha * x_ref[...] + 0.5 * o_ref[...]
        o_ref[...] = _layer_norm(y, g_ref[...], b_ref[...])


def _ffn_ln(xs, out_rows, w_ffn_in, w_ffn_out, ln_g, ln_b, layer, which, ln_idx, alpha, tm, tf):
    D = xs[0].shape[1]
    F = w_ffn_out.shape[2]
    ln_specs = [pl.BlockSpec((None, None, 1, D), lambda i, j: (layer, ln_idx, 0, 0))] * 2
    xb_scratch = pltpu.VMEM((tm, D), BF16)

    def tile_owner(rows_list):
        return [(a, t) for a, rows in enumerate(rows_list) for t in range(rows // tm)]
    runs = []
    for (sa, st), (da, dt) in zip(tile_owner([x.shape[0] for x in xs]), tile_owner(out_rows)):
        if runs and runs[-1][0] == sa and runs[-1][2] == da:
            runs[-1][4] += 1
        else:
            runs.append([sa, st, da, dt, 1])

    tf0 = tf // 2
    nf0 = F // tf0
    sa, st, da, dt, _ = runs[0]
    y0, wg_b, wu_b, wo_b = pl.pallas_call(
        functools.partial(_ffn_ln_kernel, alpha=alpha, n_carried=0, cast_weights=True),
        grid=(1, nf0),
        in_specs=[
            pl.BlockSpec((tm, D), lambda i, j: (st, 0), pipeline_mode=pl.Buffered(1)),
            pl.BlockSpec((None, None, D, tf0), lambda i, j: (layer, which, 0, j)),
            pl.BlockSpec((None, None, D, tf0), lambda i, j: (layer, which, 0, j + nf0)),
            pl.BlockSpec((None, None, tf0, D), lambda i, j: (layer, which, j, 0)),
        ] + ln_specs,
        out_specs=[
            pl.BlockSpec((tm, D), lambda i, j: (dt, 0)),
            pl.BlockSpec((D, tf0), lambda i, j: (0, j)),
            pl.BlockSpec((D, tf0), lambda i, j: (0, j)),
            pl.BlockSpec((tf0, D), lambda i, j: (j, 0)),
        ],
        out_shape=[
            jax.ShapeDtypeStruct((out_rows[da], D), F32),
            jax.ShapeDtypeStruct((D, F), BF16),
            jax.ShapeDtypeStruct((D, F), BF16),
            jax.ShapeDtypeStruct((F, D), BF16),
        ],
        scratch_shapes=[xb_scratch],
        compiler_params=_cparams(("arbitrary", "arbitrary")),
        name="ffn_ln_first",
    )(xs[sa], w_ffn_in, w_ffn_in, w_ffn_out, ln_g, ln_b)
    outs = {da: y0}
    runs[0][1] += 1
    runs[0][3] += 1
    runs[0][4] -= 1

    nf = F // tf
    for sa, st, da, dt, n in runs:
        if n == 0:
            continue
        carried = [outs[da]] if da in outs else []
        outs[da] = pl.pallas_call(
            functools.partial(_ffn_ln_kernel, alpha=alpha, n_carried=len(carried), cast_weights=False),
            grid=(n, nf),
            in_specs=[
                pl.BlockSpec((tm, D), lambda i, j, st=st: (i + st, 0)),
                pl.BlockSpec((D, tf), lambda i, j: (0, j)),
                pl.BlockSpec((D, tf), lambda i, j: (0, j)),
                pl.BlockSpec((tf, D), lambda i, j: (j, 0)),
            ] + ln_specs + [pl.BlockSpec(memory_space=pl.ANY)] * len(carried),
            out_specs=pl.BlockSpec((tm, D), lambda i, j, dt=dt: (i + dt, 0)),
            out_shape=jax.ShapeDtypeStruct((out_rows[da], D), F32),
            input_output_aliases={6: 0} if carried else {},
            scratch_shapes=[xb_scratch],
            compiler_params=_cparams(("arbitrary", "arbitrary")),
            name="ffn_ln",
        )(xs[sa], wg_b, wu_b, wo_b, ln_g, ln_b, *carried)
    return [outs[a] for a in range(len(out_rows))]


def _proj_row_prologue(x_ref, xb_ref, wa, aup_ref, ab_ref, la_ref):
    xb = x_ref[...].astype(BF16)
    xb_ref[...] = xb
    a_lo = _dot(xb, wa)
    z = _dot(a_lo.astype(BF16), aup_ref[...]) + ab_ref[...]
    la_ref[...] = (jnp.minimum(z, 0.0) - jnp.log(1.0 + jnp.exp(-jnp.abs(z)))) / GLA_TAU


def _proj_first_kernel(x_ref, wsrc_ref, wnext_ref, walo_ref, aup_ref, ab_ref, h_ref, la_ref, wm_ref,
                       wa_ref, xb_ref, *, n_shift, rank):
    j = pl.program_id(1)
    tn = wsrc_ref.shape[1]

    @pl.when(j == 0)
    def _():
        lane = lax.broadcasted_iota(jnp.int32, walo_ref.shape, 1)
        wa = jnp.where(lane < rank, walo_ref[...].astype(F32), 0.0).astype(BF16)
        wa_ref[...] = wa
        _proj_row_prologue(x_ref, xb_ref, wa, aup_ref, ab_ref, la_ref)

    def emit(w):
        wm_ref[...] = w
        h_ref[...] = _dot(xb_ref[...], w)

    @pl.when(j < n_shift)
    def _():
        both = jnp.concatenate([wsrc_ref[...], wnext_ref[...]], axis=1).astype(F32)
        emit(pltpu.roll(both, both.shape[1] - rank, axis=1)[:, :tn].astype(BF16))

    @pl.when(j >= n_shift)
    def _():
        emit(wsrc_ref[...])


def _proj_kernel(x_ref, wm_ref, wa_ref, aup_ref, ab_ref, h_in_ref, la_in_ref, h_ref, la_ref, xb_ref):
    del h_in_ref, la_in_ref
    j = pl.program_id(1)

    @pl.when(j == 0)
    def _():
        _proj_row_prologue(x_ref, xb_ref, wa_ref[...], aup_ref, ab_ref, la_ref)
        h_ref[...] = _dot(x_ref[...].astype(BF16), wm_ref[...])

    @pl.when(j > 0)
    def _():
        h_ref[...] = _dot(xb_ref[...], wm_ref[...])


def _proj(x, w_in, aup, abias, layer, tm, tn, seg):
    T, D = x.shape
    GK = aup.shape[2]
    head, mid, rank, tail = seg
    lo0 = head + mid
    NH = lo0 + tail
    tn0 = tn // 2
    assert lo0 % tn0 == 0 and head % tn0 == 0 and tail % tn0 == 0 and lo0 % V7X_LANES == 0
    n_shift, n_al = tail // tn0, lo0 // tn0
    per_tile = tn0 // V7X_LANES

    def src_block(j):
        return jnp.where(j < n_shift, n_al + j, (j - n_shift + head // tn0) % n_al)

    def next_block(j):
        return (n_al + jnp.minimum(j, n_shift - 1) + 1) * per_tile

    const = lambda i, j: (layer, 0, 0)
    h0, la0, wm, wa = pl.pallas_call(
        functools.partial(_proj_first_kernel, n_shift=n_shift, rank=rank),
        grid=(1, NH // tn0),
        in_specs=[
            pl.BlockSpec((tm, D), lambda i, j: (0, 0), pipeline_mode=pl.Buffered(1)),
            pl.BlockSpec((None, D, tn0), lambda i, j: (layer, 0, src_block(j))),
            pl.BlockSpec((None, D, V7X_LANES), lambda i, j: (layer, 0, next_block(j))),
            pl.BlockSpec((None, D, V7X_LANES), lambda i, j: (layer, 0, lo0 // V7X_LANES)),
            pl.BlockSpec((None, V7X_LANES, GK), const),
            pl.BlockSpec((None, 1, GK), const),
        ],
        out_specs=[
            pl.BlockSpec((tm, tn0), lambda i, j: (0, j)),
            pl.BlockSpec((tm, GK), lambda i, j: (0, 0)),
            pl.BlockSpec((D, tn0), lambda i, j: (0, j)),
            pl.BlockSpec((D, V7X_LANES), lambda i, j: (0, 0)),
        ],
        out_shape=[
            jax.ShapeDtypeStruct((T, NH), F32), jax.ShapeDtypeStruct((T, GK), F32),
            jax.ShapeDtypeStruct((D, NH), BF16), jax.ShapeDtypeStruct((D, V7X_LANES), BF16),
        ],
        scratch_shapes=[pltpu.VMEM((tm, D), BF16)],
        compiler_params=_cparams(("arbitrary", "arbitrary")),
        name="proj_first",
    )(x, w_in, w_in, w_in, aup, abias)

    return pl.pallas_call(
        _proj_kernel,
        grid=(T // tm - 1, NH // tn),
        in_specs=[
            pl.BlockSpec((tm, D), lambda i, j: (i + 1, 0)),
            pl.BlockSpec((D, tn), lambda i, j: (0, j)),
            pl.BlockSpec((D, V7X_LANES), lambda i, j: (0, 0)),
            pl.BlockSpec((None, V7X_LANES, GK), const),
            pl.BlockSpec((None, 1, GK), const),
            pl.BlockSpec(memory_space=pl.ANY),
            pl.BlockSpec(memory_space=pl.ANY),
        ],
        out_specs=[
            pl.BlockSpec((tm, tn), lambda i, j: (i + 1, j)),
            pl.BlockSpec((tm, GK), lambda i, j: (i + 1, 0)),
        ],
        out_shape=[jax.ShapeDtypeStruct((T, NH), F32), jax.ShapeDtypeStruct((T, GK), F32)],
        input_output_aliases={5: 0, 6: 1},
        scratch_shapes=[pltpu.VMEM((tm, D), BF16)],
        compiler_params=_cparams(("parallel", "arbitrary")),
        name="proj",
    )(x, wm, wa, aup, abias, h0, la0)


def _stack01(mats, dtype):
    return jnp.asarray(np.concatenate(mats, axis=0).astype(np.float32), dtype=dtype)


def _cumsum_mat(rows):
    i = np.arange(rows)[:, None]
    j = np.arange(rows)[None, :]
    return _stack01([j <= i], BF16)


def _seq_cumsum_mats(rows, seq):
    i = np.arange(rows)[:, None]
    j = np.arange(rows)[None, :]
    same = i // seq == j // seq
    return _stack01([same & (j <= i), same], BF16)


def _level_masks(rows, groups):
    i = np.arange(rows)[:, None]
    j = np.arange(rows)[None, :]
    out = []
    for grp in groups:
        half = grp // 2
        out.append((i // grp == j // grp) & (i % grp >= half) & (j % grp < half))
    return _stack01(out, F32)


def _intra_chunk_scores(qs, k, b, groups, lmask_ref, sub):
    C = qs.shape[0]
    A = jnp.zeros((C, C), F32)
    for n, grp in enumerate(groups):
        e = jnp.concatenate(
            [jnp.exp2(-jnp.abs(b[s:s + grp] - b[s + grp // 2 - 1:s + grp // 2])) for s in range(0, C, grp)],
            axis=0)
        A = A + _dot_nt((qs * e).astype(BF16), (k * e).astype(BF16)) * lmask_ref[n * C:(n + 1) * C, :]
    row_in_sub = lax.broadcasted_iota(jnp.int32, (C, 1), 0) & (sub - 1)
    delta = lax.broadcasted_iota(jnp.int32, (C, C), 0) - lax.broadcasted_iota(jnp.int32, (C, C), 1)
    for d in range(sub):
        if d == 0:
            t = qs * k
        else:
            kd = pltpu.roll(k, d, axis=0)
            bd = pltpu.roll(b, d, axis=0)
            t = qs * kd * jnp.exp2(jnp.minimum(b - bd, 0.0))
        red = jnp.sum(t, axis=1, keepdims=True)
        red = jnp.where(row_in_sub >= d, red, 0.0)
        A = A + jnp.where(delta == d, red, 0.0)
    return A


def _head_norm(o):
    return o * lax.rsqrt(jnp.mean(o * o, axis=-1, keepdims=True) + HEAD_NORM_EPS)


def _gla_prompt_step(t, nt, q_ref, k_ref, v_ref, la_ref, cmat_ref, lmask_ref, o_ref, sout_ref, s_ref,
                     *, scale, sub, groups):
    C = q_ref.shape[0]
    HB, DK, DV = s_ref.shape

    @pl.when(t == 0)
    def _():
        s_ref[...] = jnp.zeros_like(s_ref)

    cm = cmat_ref[...]
    p0, p1, p2 = _split3(la_ref[...] * LOG2_E)
    b_all = _dot(cm, p0) + _dot(cm, p1) + _dot(cm, p2)
    for hh in range(HB):
        kc = slice(hh * DK, (hh + 1) * DK)
        vc = slice(hh * DV, (hh + 1) * DV)
        qs = q_ref[:, kc] * scale
        k = k_ref[:, kc]
        vb = v_ref[:, vc].astype(BF16)
        b = b_all[:, kc]

        A = _intra_chunk_scores(qs, k, b, groups, lmask_ref, sub)
        S = s_ref[hh]
        o = _dot((qs * jnp.exp2(b)).astype(BF16), S.astype(BF16)) + _dot(A.astype(BF16), vb)
        o_ref[:, vc] = _head_norm(o)

        bl = b[C - 1:C, :]
        ku_t = (k * jnp.exp2(bl - b)).T
        d_s = _dot(ku_t.astype(BF16), vb)
        a_t = jnp.exp2(jnp.broadcast_to(bl, (V7X_LANES, DK))).T
        for n in range(DV // V7X_LANES):
            sl = slice(n * V7X_LANES, (n + 1) * V7X_LANES)
            s_ref[hh, :, sl] = a_t * S[:, sl] + d_s[:, sl]

    @pl.when(t == nt - 1)
    def _():
        sout_ref[...] = s_ref[...]


def _gla_sample_step(q_ref, k_ref, v_ref, la_ref, cmat_ref, sin_ref, o_ref, sout_ref, *, scale, seq):
    R, DK = q_ref.shape
    DV = v_ref.shape[1]
    nb = R // seq
    pad = V7X_LANES - R

    qs = q_ref[...] * scale
    k = k_ref[...]
    vb = v_ref[...].astype(BF16)
    cm = cmat_ref[...]
    p0, p1, p2 = _split3(la_ref[...] * LOG2_E)
    ball = _dot(cm, p0) + _dot(cm, p1) + _dot(cm, p2)
    b = ball[0:R]
    bl = ball[R:2 * R]

    A = _intra_chunk_scores(qs, k, b, (), None, seq)
    o_intra = _dot(A.astype(BF16), vb)
    qo = qs * jnp.exp2(b)
    ku = k * jnp.exp2(bl - b)
    if pad:
        ku = jnp.concatenate([ku, jnp.zeros((pad, DK), F32)], axis=0)
        vb = jnp.concatenate([vb, jnp.zeros((pad, DV), BF16)], axis=0)
    ku_t = ku.T
    lane_seq = lax.broadcasted_iota(jnp.int32, ku_t.shape, 1) // seq
    for i in range(nb):
        rows = slice(i * seq, (i + 1) * seq)
        S = sin_ref[i]
        o = _dot(qo[rows].astype(BF16), S.astype(BF16)) + o_intra[rows]
        o_ref[rows, :] = _head_norm(o)
        d_s = _dot(jnp.where(lane_seq == i, ku_t, 0.0).astype(BF16), vb)
        a_t = jnp.exp2(jnp.broadcast_to(bl[i * seq:i * seq + 1, :], (V7X_LANES, DK))).T
        for n in range(DV // V7X_LANES):
            sl = slice(n * V7X_LANES, (n + 1) * V7X_LANES)
            sout_ref[i, :, sl] = a_t * S[:, sl] + d_s[:, sl]


def _gla_kernel(*refs, n_prompt, n_sample, nt, n_carried, prompt_kw, sample_kw):
    refs = list(refs)
    p_in = [refs.pop(0) for _ in range(6)]
    s_in = [refs.pop(0) for _ in range(6)]
    del refs[:n_carried]
    p_out = [refs.pop(0) for _ in range(2)]
    s_out = [refs.pop(0) for _ in range(2)]
    (s_ref,) = refs
    s = pl.program_id(0)

    def guarded(n_steps, fn):
        if n_steps == max(n_prompt, n_sample):
            fn()
        else:
            pl.when(s < n_steps)(fn)

    guarded(n_prompt, lambda: _gla_prompt_step(s % nt, nt, *p_in, *p_out, s_ref, **prompt_kw))
    guarded(n_sample, lambda: _gla_sample_step(*s_in, *s_out, **sample_kw))


def _gla(h, la, state, new_state, prompt_states, layer, batch, seq, dbatch, dseq, heads, dk, dv,
         q_off, k_off, v_off):
    C = GLA_CHUNK_ROWS
    nt = seq // C
    groups = []
    g = C
    while g > GLA_SUB:
        groups.append(g)
        g //= 2
    cmat_p = _cumsum_mat(C)
    lmask = _level_masks(C, groups)
    wk, wv = heads * dk, heads * dv
    assert q_off % wk == 0 and k_off % wk == 0 and v_off % wv == 0
    pq, pk, pv = q_off // wk, k_off // wk, v_off // wv
    n_prompt = batch * nt

    R = GLA_SAMPLE_ROWS
    nb = R // dseq
    rb = batch * seq // R
    cmat_s = _seq_cumsum_mats(R, dseq)
    sq, sk, sv = q_off // dk, k_off // dk, v_off // dv
    n_sample = (dbatch // nb) * heads
    n_steps = max(n_prompt, n_sample)

    def pmap(fn):
        return lambda s: fn(jnp.minimum(s, n_prompt - 1))

    def smap(fn):
        return lambda s: fn(jnp.minimum(s, n_sample - 1) // heads, jnp.minimum(s, n_sample - 1) % heads)

    carried = [] if new_state is None else [new_state, prompt_states]
    n_layers = state.shape[0]
    return pl.pallas_call(
        functools.partial(
            _gla_kernel, n_prompt=n_prompt, n_sample=n_sample, nt=nt, n_carried=len(carried),
            prompt_kw=dict(scale=float(dk) ** -0.5, sub=GLA_SUB, groups=tuple(groups)),
            sample_kw=dict(scale=float(dk) ** -0.5, seq=dseq)),
        grid=(n_steps,),
        in_specs=[
            pl.BlockSpec((C, wk), pmap(lambda c: (c, pq))),
            pl.BlockSpec((C, wk), pmap(lambda c: (c, pk))),
            pl.BlockSpec((C, wv), pmap(lambda c: (c, pv))),
            pl.BlockSpec((C, wk), pmap(lambda c: (c, 0))),
            pl.BlockSpec(cmat_p.shape, lambda s: (0, 0)),
            pl.BlockSpec(lmask.shape, lambda s: (0, 0)),
            pl.BlockSpec((R, dk), smap(lambda r, hi: (rb + r, sq + hi))),
            pl.BlockSpec((R, dk), smap(lambda r, hi: (rb + r, sk + hi))),
            pl.BlockSpec((R, dv), smap(lambda r, hi: (rb + r, sv + hi))),
            pl.BlockSpec((R, dk), smap(lambda r, hi: (rb + r, hi))),
            pl.BlockSpec(cmat_s.shape, lambda s: (0, 0)),
            pl.BlockSpec((None, nb, None, dk, dv), smap(lambda r, hi: (layer, r, hi, 0, 0))),
        ] + [pl.BlockSpec(memory_space=pl.ANY)] * len(carried),
        out_specs=[
            pl.BlockSpec((C, wv), pmap(lambda c: (c, 0))),
            pl.BlockSpec((None, None, heads, dk, dv), pmap(lambda c: (layer, c // nt, 0, 0, 0))),
            pl.BlockSpec((R, dv), smap(lambda r, hi: (r, hi))),
            pl.BlockSpec((None, nb, None, dk, dv), smap(lambda r, hi: (layer, r, hi, 0, 0))),
        ],
        out_shape=[
            jax.ShapeDtypeStruct((batch * seq, heads * dv), F32),
            jax.ShapeDtypeStruct((n_layers, batch, heads, dk, dv), F32),
            jax.ShapeDtypeStruct((dbatch * dseq, heads * dv), F32),
            jax.ShapeDtypeStruct(state.shape, F32),
        ],
        input_output_aliases={12: 3, 13: 1} if carried else {},
        scratch_shapes=[pltpu.VMEM((heads, dk, dv), F32)],
        compiler_params=_cparams(("arbitrary",)),
        name="gla",
    )(h, h, h, la, cmat_p, lmask, h, h, h, la, cmat_s, state, *carried)


def _window_means_minus_input(ext, u, pos, lead):
    n, pw = u.shape
    gw = pw // len(POOL_WINDOWS)
    outs = []
    for g, w in enumerate(POOL_WINDOWS):
        cols = slice(g * gw, (g + 1) * gw)
        p = ext[:, cols]
        span = 1
        while span < w:
            p = p + pltpu.roll(p, span, axis=0)
            span *= 2
        cnt = jnp.minimum(w, pos + 1).astype(F32)
        outs.append(p[lead:lead + n] / cnt - u[:, cols])
    return outs


def _mix_tail(pooled, on, r, ga, gb, x, hg_ref, wgo_ref, pw_ref, ps_ref, wo_ref, g_ref, b_ref, alpha):
    ya = jnp.concatenate(
        [_dot(p.astype(BF16), pw_ref[g]) for g, p in enumerate(pooled)], axis=1) * ps_ref[...]
    yb = _dot((on * hg_ref[...] * (r * jax.nn.sigmoid(r))).astype(BF16), wgo_ref[...])
    merged = jax.nn.sigmoid(ga) * ya + jax.nn.sigmoid(gb) * yb
    m = _dot(merged.astype(BF16), wo_ref[...])
    return _layer_norm(alpha * x + m, g_ref[...], b_ref[...])


def _post_prompt_kernel(on_ref, r_ref, ga_ref, gb_ref, u_ref, x_ref, hg_ref, wgo_ref, pw_ref, ps_ref,
                        wo_ref, g_ref, b_ref, o_ref, npool_ref, halo_ref, *, tiles_per_seq, alpha):
    i = pl.program_id(0)
    tm = u_ref.shape[0]
    lead = halo_ref.shape[0]
    ti = i % tiles_per_seq

    @pl.when(ti == 0)
    def _():
        halo_ref[...] = jnp.zeros_like(halo_ref)

    u = u_ref[...]
    ext = jnp.concatenate([halo_ref[...], u], axis=0)
    halo_ref[...] = u[tm - lead:, :]

    @pl.when(ti == tiles_per_seq - 1)
    def _():
        npool_ref[...] = u[tm - POOL_BUF:, :]

    pos = ti * tm + lax.broadcasted_iota(jnp.int32, (tm, 1), 0)
    pooled = _window_means_minus_input(ext, u, pos, lead)
    o_ref[...] = _mix_tail(pooled, on_ref[...], r_ref[...], ga_ref[...], gb_ref[...], x_ref[...],
                           hg_ref, wgo_ref, pw_ref, ps_ref, wo_ref, g_ref, b_ref, alpha)


def _post_sample_kernel(on_ref, r_ref, ga_ref, gb_ref, u_ref, x_ref, st_ref, hg_ref, wgo_ref, pw_ref,
                        ps_ref, wo_ref, g_ref, b_ref, *rest, seq, start, alpha):
    o_ref, npool_ref, ext_ref = rest[-3:]
    tm, pw = u_ref.shape
    nb = tm // seq
    lead = ext_ref.shape[1] - seq
    u = u_ref[...]
    u3 = u.reshape(nb, seq, pw)
    st = st_ref[...]
    ext_ref[:, 0:lead - POOL_BUF, :] = jnp.zeros((nb, lead - POOL_BUF, pw), F32)
    ext_ref[:, lead - POOL_BUF:lead, :] = st
    ext_ref[:, lead:, :] = u3
    e3 = ext_ref[...]
    npool_ref[...] = e3[:, lead + seq - POOL_BUF:, :]
    p = e3.reshape(nb * (lead + seq), pw)
    gw = pw // len(POOL_WINDOWS)
    pos = start + lax.broadcasted_iota(jnp.int32, (nb, seq, 1), 1)
    pooled = []
    for g, w in enumerate(POOL_WINDOWS):
        cols = slice(g * gw, (g + 1) * gw)
        pg = p[:, cols]
        span = 1
        while span < w:
            pg = pg + pltpu.roll(pg, span, axis=0)
            span *= 2
        win = pg.reshape(nb, lead + seq, gw)[:, lead:, :]
        cnt = jnp.minimum(w, pos + 1).astype(F32)
        pooled.append((win / cnt - u3[:, :, cols]).reshape(tm, gw))
    o_ref[...] = _mix_tail(pooled, on_ref[...], r_ref[...], ga_ref[...], gb_ref[...], x_ref[...],
                           hg_ref, wgo_ref, pw_ref, ps_ref, wo_ref, g_ref, b_ref, alpha)


def _const_spec(shape, index_map):
    return pl.BlockSpec(shape, index_map, pipeline_mode=pl.Buffered(1))


def _post_weight_specs(layer, ln_idx, D, GV, groups, gw, ogw, n):
    z = (0,) * (n - 1)
    return [
        _const_spec((None, 1, GV), lambda *a: (layer, 0, 0)),
        _const_spec((None, GV, D), lambda *a: (layer, 0, 0)),
        _const_spec((None, groups, gw, ogw), lambda *a: (layer, 0, 0, 0)),
        _const_spec((None, 1, D), lambda *a: (layer, 0, 0)),
        _const_spec((None, D, D), lambda *a: (layer, 0, 0)),
        _const_spec((None, None, 1, D), lambda *a: (layer, ln_idx, 0, 0)),
        _const_spec((None, None, 1, D), lambda *a: (layer, ln_idx, 0, 0)),
    ]


def _post_prompt(on, h, x, weights, layer, ln_idx, alpha, seq, tm, cols):
    TP, GV = on.shape
    T, D = x.shape
    hg, wgo, pw, ps, wo, ln_g, ln_b = weights
    groups, gw, ogw = pw.shape[1:]
    PW = groups * gw
    tiles_per_seq = seq // tm
    r_b, ga_b, gb_b, u_b = cols["r"] // GV, cols["ga"] // D, cols["gb"] // D, cols["u"] // PW
    lead = 2 * V7X_SUBLANES
    return pl.pallas_call(
        functools.partial(_post_prompt_kernel, tiles_per_seq=tiles_per_seq, alpha=alpha),
        grid=(TP // tm,),
        in_specs=[
            pl.BlockSpec((tm, GV), lambda i: (i, 0)),
            pl.BlockSpec((tm, GV), lambda i: (i, r_b)),
            pl.BlockSpec((tm, D), lambda i: (i, ga_b)),
            pl.BlockSpec((tm, D), lambda i: (i, gb_b)),
            pl.BlockSpec((tm, PW), lambda i: (i, u_b)),
            pl.BlockSpec((tm, D), lambda i: (i, 0)),
        ] + _post_weight_specs(layer, ln_idx, D, GV, groups, gw, ogw, 1),
        out_specs=[
            pl.BlockSpec((tm, D), lambda i: (i, 0)),
            pl.BlockSpec((None, POOL_BUF, PW), lambda i: (i // tiles_per_seq, 0, 0)),
        ],
        out_shape=[
            jax.ShapeDtypeStruct((T, D), F32),
            jax.ShapeDtypeStruct((TP // seq, POOL_BUF, PW), F32),
        ],
        scratch_shapes=[pltpu.VMEM((lead, PW), F32)],
        compiler_params=_cparams(("arbitrary",)),
        name="post_prompt",
    )(on, h, h, h, h, x, hg, wgo, pw, ps, wo, ln_g, ln_b)


def _post_sample(on, h, x, x2, new_pool, state_pool, weights, layer, ln_idx, alpha, row0, seq, tm, cols):
    carried = [x2] if new_pool is None else [x2, new_pool]
    TS, GV = on.shape
    D = x.shape[1]
    hg, wgo, pw, ps, wo, ln_g, ln_b = weights
    groups, gw, ogw = pw.shape[1:]
    PW = groups * gw
    nb = tm // seq
    batch = TS // seq
    rb = row0 // tm
    r_b, ga_b, gb_b, u_b = cols["r"] // GV, cols["ga"] // D, cols["gb"] // D, cols["u"] // PW
    lead = 2 * V7X_SUBLANES
    return pl.pallas_call(
        functools.partial(_post_sample_kernel, seq=seq, start=PAST_LEN, alpha=alpha),
        grid=(TS // tm,),
        in_specs=[
            pl.BlockSpec((tm, GV), lambda i: (i, 0)),
            pl.BlockSpec((tm, GV), lambda i: (rb + i, r_b)),
            pl.BlockSpec((tm, D), lambda i: (rb + i, ga_b)),
            pl.BlockSpec((tm, D), lambda i: (rb + i, gb_b)),
            pl.BlockSpec((tm, PW), lambda i: (rb + i, u_b)),
            pl.BlockSpec((tm, D), lambda i: (rb + i, 0)),
            pl.BlockSpec((None, nb, POOL_BUF, PW), lambda i: (layer, i, 0, 0)),
        ] + _post_weight_specs(layer, ln_idx, D, GV, groups, gw, ogw, 1) + [
            pl.BlockSpec(memory_space=pl.ANY),
        ] * len(carried),
        out_specs=[
            pl.BlockSpec((tm, D), lambda i: (rb + i, 0)),
            pl.BlockSpec((None, nb, POOL_BUF, PW), lambda i: (layer, i, 0, 0)),
        ],
        out_shape=[
            jax.ShapeDtypeStruct(x2.shape, F32),
            jax.ShapeDtypeStruct((state_pool.shape[0], batch, POOL_BUF, PW), F32),
        ],
        input_output_aliases={14 + n: n for n in range(len(carried))},
        scratch_shapes=[pltpu.VMEM((nb, lead + seq, PW), F32)],
        compiler_params=_cparams(("arbitrary",)),
        name="post_sample",
    )(on, h, h, h, h, x, state_pool, hg, wgo, pw, ps, wo, ln_g, ln_b, *carried)


def _tiles(T, TP, seq_p, F, NH):
    def fit(n, t):
        while n % t:
            t //= 2
        return t
    return dict(
        ffn_tm=fit(np.gcd(T, TP), 1024), ffn_tf=fit(F, 512),
        proj_tm=fit(np.gcd(T, TP), 1024), proj_tn=fit(NH, 1024),
        post_tm=fit(seq_p, 256), post_sample_tm=fit(np.gcd(T - TP, TP), 256),
    )


def kernel(x_prompt, x_sample, state_pool, state_gla, ln_g, ln_b, w_ffn_in, w_ffn_out, w_in, pool_w,
           pool_scale, a_up, a_bias, head_g, w_gla_out, w_out):
    B, SEQ, D = x_prompt.shape
    DB, DSEQ, _ = x_sample.shape
    L = w_in.shape[0]
    PW = state_pool.shape[-1]
    H, DK, DV = state_gla.shape[2:]
    GK, GV = H * DK, H * DV
    RANK = a_up.shape[1]
    F = w_ffn_out.shape[2]
    TP, TS = B * SEQ, DB * DSEQ
    T = TP + TS
    alpha = float((2 * L) ** 0.25)
    assert SEQ % GLA_CHUNK_ROWS == 0 and TP % V7X_LANES == 0 and V7X_LANES % DSEQ == 0
    assert DSEQ == V7X_SUBLANES and PAST_LEN >= POOL_BUF

    seg = (PW + 2 * GK, GV, RANK, GV + 2 * D)
    assert sum(seg) == w_in.shape[2]
    w_in_b = w_in.astype(BF16)
    aup = jnp.pad(a_up, ((0, 0), (0, V7X_LANES - RANK), (0, 0))).astype(BF16)
    abias = a_bias.reshape(L, 1, GK)
    ln_g4 = ln_g.reshape(L, 3, 1, D)
    ln_b4 = ln_b.reshape(L, 3, 1, D)
    post_w = (head_g.reshape(L, 1, GV), w_gla_out.astype(BF16), pool_w.astype(BF16),
              pool_scale.reshape(L, 1, D), w_out.astype(BF16), ln_g4, ln_b4)
    c_v = GV + 2 * D
    c_u = c_v + GV
    cols = dict(r=0, ga=GV, gb=GV + D, v=c_v, u=c_u, q=c_u + PW, k=c_u + PW + GK)
    ts = _tiles(T, TP, SEQ, F, w_in.shape[2] - RANK)
    assert GV == D and c_v % DV == 0 and c_u % PW == 0 and cols["q"] % DK == 0

    ffn = functools.partial(_ffn_ln, w_ffn_in=w_ffn_in, w_ffn_out=w_ffn_out, ln_g=ln_g4, ln_b=ln_b4,
                            alpha=alpha, tm=ts["ffn_tm"], tf=ts["ffn_tf"])

    xs = [x_prompt.reshape(TP, D), x_sample.reshape(TS, D)]
    pool_p, gla_p, pool_s, gla_s = [], None, None, None
    for l in range(L):
        (x,) = ffn(xs, [T], layer=l, which=0, ln_idx=0)
        h, la = _proj(x, w_in_b, aup, abias, l, ts["proj_tm"], ts["proj_tn"], seg)
        on_p, gla_p, on_s, gla_s = _gla(h, la, state_gla, gla_s, gla_p, l, B, SEQ, DB, DSEQ, H, DK, DV,
                                        cols["q"], cols["k"], cols["v"])
        x2, np_p = _post_prompt(on_p, h, x, post_w, l, 1, alpha, SEQ, ts["post_tm"], cols)
        x2, pool_s = _post_sample(on_s, h, x, x2, pool_s, state_pool, post_w, l, 1, alpha, TP, DSEQ,
                                  ts["post_sample_tm"], cols)
        xs = ffn([x2], [T] if l + 1 < L else [TP, TS], layer=l, which=1, ln_idx=2)
        pool_p.append(np_p)
    return (xs[0].reshape(B, SEQ, D), xs[1].reshape(DB, DSEQ, D), jnp.stack(pool_p),
            gla_p, pool_s, gla_s)
```
